```python
import jax, jax.numpy as jnp
from jax import lax
import numpy as np

D_MODEL = 1024
BATCH = 32
SEQ = 256
DEPTH = 2
DEC_BATCH = 4
DEC_SEQ = 1024
PAST_LEN = 256

GRID_W = 64
N_MIXERS = 2
N_CONV_LAYERS = (DEPTH + 1) // 2
N_ATTN_LAYERS = DEPTH // 2
CONV_WIDTH = 31
CONV_PAD = CONV_WIDTH // 2
N_HEADS = 16
N_KV_HEADS = 4
HEAD_DIM = 64
GQA_GROUP = N_HEADS // N_KV_HEADS
ATTN_WIDTH = N_HEADS * HEAD_DIM
WINDOW = 128
BAND_BLOCK = WINDOW
ROPE_THETA = 10000.0
ROPE_PAIRS = HEAD_DIM // 4
N_EXPERTS = 64
N_EXPERT_GROUPS = 8
TOPK_GROUPS = 4
TOP_K = 8
D_EXPERT = 256
D_SHARED = 256
ROUTED_SCALE = 2.5
MOE_BLOCK = 128
NORM_EPS = 1e-6
N_ADA = 6

kernel_name = 'hybrid_conv_swa_moe_diffusion_step'


def rms_norm(x, g):
    xf = x.astype(jnp.float32)
    y = xf * lax.rsqrt(jnp.mean(xf * xf, axis=-1, keepdims=True) + NORM_EPS)
    return (y * g.astype(jnp.float32)).astype(x.dtype)


def ada_params(cvec, w, b):
    m = jax.nn.silu(cvec) @ w + b
    return jnp.split(m, N_ADA, axis=-1)


def modulate(x, g, shift, scale):
    return rms_norm(x, g) * (1.0 + scale[:, None, :]) + shift[:, None, :]


def swiglu(x, wg, wu, wd):
    return (jax.nn.silu(x @ wg) * (x @ wu)) @ wd


def conv_module(h, w1, b1, wdw, bdw, gn, w2, b2):
    u = h @ w1 + b1
    a, gt = jnp.split(u, 2, axis=-1)
    u = a * jax.nn.sigmoid(gt)
    dw = lax.conv_general_dilated(u, wdw[:, None, :], window_strides=(1,),
                                  padding=[(CONV_PAD, CONV_PAD)],
                                  dimension_numbers=('NWC', 'WIO', 'NWC'),
                                  feature_group_count=u.shape[-1]) + bdw
    return jax.nn.silu(rms_norm(dw, gn)) @ w2 + b2


def rope2d(x):
    L = x.shape[1]
    rows = L // GRID_W
    row = jnp.repeat(jnp.arange(rows, dtype=jnp.float32), GRID_W)
    col = jnp.tile(jnp.arange(GRID_W, dtype=jnp.float32), rows)
    inv_freq = ROPE_THETA ** (-jnp.arange(ROPE_PAIRS, dtype=jnp.float32) / ROPE_PAIRS)

    def rot(y, pos):
        ang = pos[:, None] * inv_freq[None, :]
        cos = jnp.cos(ang)[None, :, None, :]
        sin = jnp.sin(ang)[None, :, None, :]
        yf = y.astype(jnp.float32)
        y1, y2 = yf[..., :ROPE_PAIRS], yf[..., ROPE_PAIRS:]
        return jnp.concatenate([y1 * cos - y2 * sin, y1 * sin + y2 * cos], axis=-1)

    half = HEAD_DIM // 2
    out = jnp.concatenate([rot(x[..., :half], row), rot(x[..., half:], col)], axis=-1)
    return out.astype(x.dtype)


def qkv_heads(h, wqkv, qn, kn):
    B, L, _ = h.shape
    qkv = h @ wqkv
    q = qkv[..., :ATTN_WIDTH].reshape(B, L, N_HEADS, HEAD_DIM)
    k = qkv[..., ATTN_WIDTH:ATTN_WIDTH + N_KV_HEADS * HEAD_DIM].reshape(B, L, N_KV_HEADS, HEAD_DIM)
    v = qkv[..., ATTN_WIDTH + N_KV_HEADS * HEAD_DIM:].reshape(B, L, N_KV_HEADS, HEAD_DIM)
    return rms_norm(q, qn), rms_norm(k, kn), v


def attn_context(h, wqkv, qn, kn, sink, wo):
    B, L, _ = h.shape
    q, k, v = qkv_heads(h, wqkv, qn, kn)
    qg = q.reshape(B, L, N_KV_HEADS, GQA_GROUP, HEAD_DIM)
    s = jnp.einsum('bqkgd,bskd->bkgqs', qg, k, preferred_element_type=jnp.float32) * (HEAD_DIM ** -0.5)
    s_sink = jnp.broadcast_to(sink.astype(jnp.float32).reshape(N_KV_HEADS, GQA_GROUP)[None, :, :, None, None],
                              s.shape[:-1] + (1,))
    p = jax.nn.softmax(jnp.concatenate([s, s_sink], axis=-1), axis=-1)[..., :L]
    o = jnp.einsum('bkgqs,bskd->bqkgd', p.astype(v.dtype), v).reshape(B, L, ATTN_WIDTH)
    return o @ wo, k, v


def band_mask(nb, L):
    a = jnp.arange(BAND_BLOCK)[:, None]
    b = jnp.arange(3 * BAND_BLOCK)[None, :]
    rel = b - BAND_BLOCK - a
    kpos = (jnp.arange(nb)[:, None, None] - 1) * BAND_BLOCK + b[None]
    return (jnp.abs(rel) <= WINDOW)[None] & (kpos >= 0) & (kpos < L)


def attn_latent(h, k_ctx, v_ctx, wqkv, qn, kn, sink, wo):
    B, L, _ = h.shape
    q, k, v = qkv_heads(h, wqkv, qn, kn)
    q, k = rope2d(q), rope2d(k)
    nb = L // BAND_BLOCK
    pad = ((0, 0), (BAND_BLOCK, BAND_BLOCK), (0, 0), (0, 0))
    kb = jnp.pad(k, pad).reshape(B, nb + 2, BAND_BLOCK, N_KV_HEADS, HEAD_DIM)
    vb = jnp.pad(v, pad).reshape(B, nb + 2, BAND_BLOCK, N_KV_HEADS, HEAD_DIM)
    k_band = jnp.concatenate([kb[:, :-2], kb[:, 1:-1], kb[:, 2:]], axis=2)
    v_band = jnp.concatenate([vb[:, :-2], vb[:, 1:-1], vb[:, 2:]], axis=2)
    qb = q.reshape(B, nb, BAND_BLOCK, N_KV_HEADS, GQA_GROUP, HEAD_DIM)
    scale = HEAD_DIM ** -0.5
    s_loc = jnp.einsum('bnqkgd,bnskd->bnkgqs', qb, k_band, preferred_element_type=jnp.float32) * scale
    mask = band_mask(nb, L)[None, :, None, None]
    s_loc = jnp.where(mask, s_loc, -jnp.inf)
    s_ctx = jnp.einsum('bnqkgd,bckd->bnkgqc', qb, k_ctx, preferred_element_type=jnp.float32) * scale
    s_sink = jnp.broadcast_to(sink.astype(jnp.float32).reshape(N_KV_HEADS, GQA_GROUP)[None, None, :, :, None, None],
                              s_loc.shape[:-1] + (1,))
    p = jax.nn.softmax(jnp.concatenate([s_loc, s_ctx, s_sink], axis=-1), axis=-1)
    nl = 3 * BAND_BLOCK
    lc = k_ctx.shape[1]
    p_loc = p[..., :nl].astype(v.dtype)
    p_ctx = p[..., nl:nl + lc].astype(v.dtype)
    o = (jnp.einsum('bnkgqs,bnskd->bnqkgd', p_loc, v_band)
         + jnp.einsum('bnkgqc,bckd->bnqkgd', p_ctx, v_ctx.astype(v.dtype)))
    return o.reshape(B, L, ATTN_WIDTH) @ wo


def routed_experts(x, eidx, wsel, wg, wu, wd):
    T, D = x.shape
    tk = T * TOP_K
    n_blocks = -(-tk // MOE_BLOCK) + N_EXPERTS
    e_flat = eidx.reshape(-1)
    tok_flat = jnp.repeat(jnp.arange(T, dtype=jnp.int32), TOP_K)
    w_flat = wsel.reshape(-1)
    order = jnp.argsort(e_flat)
    e_sorted = e_flat[order]
    counts = jnp.bincount(e_flat, length=N_EXPERTS)
    padded = (counts + MOE_BLOCK - 1) // MOE_BLOCK * MOE_BLOCK
    start = jnp.cumsum(counts) - counts
    pad_end = jnp.cumsum(padded)
    pad_start = pad_end - padded
    dest = pad_start[e_sorted] + jnp.arange(tk) - start[e_sorted]
    slot_tok = jnp.full((n_blocks * MOE_BLOCK,), T, jnp.int32).at[dest].set(tok_flat[order])
    slot_w = jnp.zeros((n_blocks * MOE_BLOCK,), jnp.float32).at[dest].set(w_flat[order])
    block_exp = jnp.minimum(jnp.searchsorted(pad_end, jnp.arange(n_blocks) * MOE_BLOCK, side='right'),
                            N_EXPERTS - 1)
    x_pad = jnp.concatenate([x, jnp.zeros((1, D), x.dtype)], axis=0)
    xs = x_pad[slot_tok].reshape(n_blocks, MOE_BLOCK, D)

    def expert_block(args):
        xb, e = args
        return swiglu(xb, wg[e], wu[e], wd[e])

    ys = lax.map(expert_block, (xs, block_exp)).reshape(-1, D)
    out = jnp.zeros((T + 1, D), x.dtype).at[slot_tok].add(ys * slot_w[:, None].astype(x.dtype))
    return out[:T]


def moe(h, router_w, router_bias, wg, wu, wd, sg, su, sd):
    B, L, D = h.shape
    x = h.reshape(B * L, D)
    T = B * L
    scores = jax.nn.sigmoid(jnp.einsum('td,de->te', x, router_w, preferred_element_type=jnp.float32))
    choice = scores + router_bias.astype(jnp.float32)
    grp = choice.reshape(T, N_EXPERT_GROUPS, N_EXPERTS // N_EXPERT_GROUPS)
    gscore = lax.top_k(grp, 2)[0].sum(-1)
    _, gidx = lax.top_k(gscore, TOPK_GROUPS)
    gmask = jax.nn.one_hot(gidx, N_EXPERT_GROUPS, dtype=jnp.float32).sum(1) > 0
    emask = jnp.repeat(gmask, N_EXPERTS // N_EXPERT_GROUPS, axis=1)
    _, eidx = lax.top_k(jnp.where(emask, choice, -jnp.inf), TOP_K)
    wsel = jnp.take_along_axis(scores, eidx, axis=1)
    wsel = wsel / jnp.sum(wsel, axis=-1, keepdims=True) * ROUTED_SCALE
    routed = routed_experts(x, eidx, wsel, wg, wu, wd)
    return (routed + swiglu(x, sg, su, sd)).reshape(B, L, D)


def setup_inputs(seed: int = 0) -> dict:
    key = jax.random.key(seed)
    ks = jax.random.split(key, 32)
    f32 = jnp.float32
    D = D_MODEL
    nrm = lambda k, shape, s: jax.random.normal(k, shape, f32) * s
    return {
        'x_prompt': nrm(ks[0], (BATCH, SEQ, D), 1.0),
        'x_sample': nrm(ks[1], (DEC_BATCH, DEC_SEQ, D), 1.0),
        'cache_k': nrm(ks[2], (DEC_BATCH, N_ATTN_LAYERS, PAST_LEN, N_KV_HEADS, HEAD_DIM), 1.0),
        'cache_v': nrm(ks[3], (DEC_BATCH, N_ATTN_LAYERS, PAST_LEN, N_KV_HEADS, HEAD_DIM), 1.0),
        'c': nrm(ks[4], (DEC_BATCH, D), 1.0),
        'c_ctx': nrm(ks[5], (D,), 1.0),
        'ada_w': nrm(ks[6], (DEPTH, D, N_ADA * D), 0.5 * D ** -0.5),
        'ada_b': nrm(ks[7], (DEPTH, N_ADA * D), 0.02),
        'norm_mix': 1.0 + nrm(ks[8], (DEPTH, D), 0.01),
        'norm_ffn': 1.0 + nrm(ks[9], (DEPTH, D), 0.01),
        'conv_w1': nrm(ks[10], (N_CONV_LAYERS, D, 2 * D), D ** -0.5),
        'conv_b1': nrm(ks[11], (N_CONV_LAYERS, 2 * D), 0.02),
        'conv_dw': nrm(ks[12], (N_CONV_LAYERS, CONV_WIDTH, D), CONV_WIDTH ** -0.5),
        'conv_dw_b': nrm(ks[13], (N_CONV_LAYERS, D), 0.02),
        'conv_norm': 1.0 + nrm(ks[14], (N_CONV_LAYERS, D), 0.01),
        'conv_w2': nrm(ks[15], (N_CONV_LAYERS, D, D), D ** -0.5),
        'conv_b2': nrm(ks[16], (N_CONV_LAYERS, D), 0.02),
        'attn_wqkv': nrm(ks[17], (N_ATTN_LAYERS, D, ATTN_WIDTH + 2 * N_KV_HEADS * HEAD_DIM), D ** -0.5),
        'attn_q_norm': 1.0 + nrm(ks[18], (N_ATTN_LAYERS, HEAD_DIM), 0.01),
        'attn_k_norm': 1.0 + nrm(ks[19], (N_ATTN_LAYERS, HEAD_DIM), 0.01),
        'attn_sink': nrm(ks[20], (N_ATTN_LAYERS, N_HEADS), 0.5),
        'attn_wo': nrm(ks[21], (N_ATTN_LAYERS, ATTN_WIDTH, D), ATTN_WIDTH ** -0.5),
        'router_w': nrm(ks[22], (DEPTH, D, N_EXPERTS), D ** -0.5),
        'router_bias': nrm(ks[23], (DEPTH, N_EXPERTS), 0.01),
        'exp_w_gate': nrm(ks[24], (DEPTH, N_EXPERTS, D, D_EXPERT), D ** -0.5),
        'exp_w_up': nrm(ks[25], (DEPTH, N_EXPERTS, D, D_EXPERT), D ** -0.5),
        'exp_w_down': nrm(ks[26], (DEPTH, N_EXPERTS, D_EXPERT, D), D_EXPERT ** -0.5),
        'sh_w_gate': nrm(ks[27], (DEPTH, D, D_SHARED), D ** -0.5),
        'sh_w_up': nrm(ks[28], (DEPTH, D, D_SHARED), D ** -0.5),
        'sh_w_down': nrm(ks[29], (DEPTH, D_SHARED, D), D_SHARED ** -0.5),
    }


def reference(x_prompt, x_sample, cache_k, cache_v, c, c_ctx,
              ada_w, ada_b, norm_mix, norm_ffn,
              conv_w1, conv_b1, conv_dw, conv_dw_b, conv_norm, conv_w2, conv_b2,
              attn_wqkv, attn_q_norm, attn_k_norm, attn_sink, attn_wo,
              router_w, router_bias, exp_w_gate, exp_w_up, exp_w_down,
              sh_w_gate, sh_w_up, sh_w_down):
    y_p = x_prompt
    y_s = x_sample
    k_list = []
    v_list = []
    for i in range(DEPTH):
        sh1_p, sc1_p, g1_p, sh2_p, sc2_p, g2_p = ada_params(c_ctx[None, :], ada_w[i], ada_b[i])
        sh1_s, sc1_s, g1_s, sh2_s, sc2_s, g2_s = ada_params(c, ada_w[i], ada_b[i])
        hp = modulate(y_p, norm_mix[i], sh1_p, sc1_p)
        hs = modulate(y_s, norm_mix[i], sh1_s, sc1_s)
        j = i // N_MIXERS
        if i % N_MIXERS == 0:
            cp = (conv_w1[j], conv_b1[j], conv_dw[j], conv_dw_b[j], conv_norm[j], conv_w2[j], conv_b2[j])
            mp = conv_module(hp, *cp)
            ms = conv_module(hs, *cp)
        else:
            ap = (attn_wqkv[j], attn_q_norm[j], attn_k_norm[j], attn_sink[j], attn_wo[j])
            mp, kc, vc = attn_context(hp, *ap)
            k_list.append(kc)
            v_list.append(vc)
            ms = attn_latent(hs, cache_k[:, j], cache_v[:, j], *ap)
        y_p = y_p + g1_p[:, None, :] * mp
        y_s = y_s + g1_s[:, None, :] * ms
        mw = (router_w[i], router_bias[i], exp_w_gate[i], exp_w_up[i], exp_w_down[i],
              sh_w_gate[i], sh_w_up[i], sh_w_down[i])
        y_p = y_p + g2_p[:, None, :] * moe(modulate(y_p, norm_ffn[i], sh2_p, sc2_p), *mw)
        y_s = y_s + g2_s[:, None, :] * moe(modulate(y_s, norm_ffn[i], sh2_s, sc2_s), *mw)
    new_k = jnp.stack(k_list, axis=1)
    new_v = jnp.stack(v_list, axis=1)
    return (y_p, y_s, new_k, new_v)
```

```python
import functools

import jax
import jax.numpy as jnp
from jax import lax
from jax.experimental import pallas as pl
from jax.experimental.pallas import tpu as pltpu

F32 = jnp.float32
BF16 = jnp.bfloat16
I32 = jnp.int32

D = 1024
N_CTX_SEQ = 32
CTX_LEN = 256
N_LAT_SEQ = 4
LAT_LEN = 1024
TP = N_CTX_SEQ * CTX_LEN
TS = N_LAT_SEQ * LAT_LEN
T = TP + TS
DEPTH = 2
N_ADA = 6
N_COND = 1 + N_LAT_SEQ
COND_ROWS = 8
CONV_WIDTH = 31
CONV_PAD = CONV_WIDTH // 2
N_HEADS = 16
N_KV = 4
HEAD_DIM = 64
GQA = N_HEADS // N_KV
KV_WIDTH = N_KV * HEAD_DIM
QKV_WIDTH = D + 2 * KV_WIDTH
WINDOW = 128
GRID_W = 64
ROPE_PAIRS = HEAD_DIM // 4
ROPE_THETA = 10000.0
N_EXPERTS = 64
N_EGROUPS = 8
EGROUP = N_EXPERTS // N_EGROUPS
TOPK_GROUPS = 4
TOP_K = 8
D_EXPERT = 256
D_SHARED = 256
ROUTED_SCALE = 2.5
NORM_EPS = 1e-6

LANES = 128
SUBLANES = 8
VMEM_LIMIT = 56 * 1024 * 1024

ROW_TILE = 256
N_TILES = T // ROW_TILE
N_CTX_TILES = TP // ROW_TILE
LAT_TILES_PER_SEQ = LAT_LEN // ROW_TILE
HALO = 16
LANE_CHUNKS = D // LANES

MOE_GROUP = 4096
N_GROUPS = T // MOE_GROUP
TILES_PER_GROUP = MOE_GROUP // ROW_TILE
MOE_BM = 256
MOE_NB = MOE_GROUP * TOP_K // MOE_BM + N_EXPERTS
MOE_SLOTS = MOE_NB * MOE_BM
N_ASSIGN = MOE_GROUP * TOP_K
DUMMY_ASSIGN = N_ASSIGN
ACC_ROWS = MOE_GROUP + SUBLANES
RANK_BITS = 12
Q_BLOCK = 128


def _cond_of_tile(i):
    return jnp.where(i < N_CTX_TILES, 0, 1 + (i - N_CTX_TILES) // LAT_TILES_PER_SEQ)


def _params(sem, vmem=VMEM_LIMIT):
    return pltpu.CompilerParams(dimension_semantics=sem, vmem_limit_bytes=vmem)


def _bdot(a, b):
    return jnp.dot(a.astype(BF16), b.astype(BF16), preferred_element_type=F32)


def _split(a):
    hi = a.astype(BF16)
    lo = (a - hi.astype(F32)).astype(BF16)
    return hi, lo


def _dot3(a, b):
    a_hi, a_lo = _split(a)
    b_hi, b_lo = _split(b)
    d = functools.partial(jnp.dot, preferred_element_type=F32)
    return d(a_hi, b_hi) + d(a_lo, b_hi) + d(a_hi, b_lo)


def _rms(x, g):
    return x * lax.rsqrt(jnp.mean(x * x, axis=-1, keepdims=True) + NORM_EPS) * g


def _modulate(x, g, shift, scale):
    return _rms(x, g) * (1.0 + scale) + shift


def _silu(x):
    return x * jax.nn.sigmoid(x)


def _ada_spec(layer):
    return pl.BlockSpec((None, None, N_ADA, D), lambda i: (layer, _cond_of_tile(i), 0, 0))


def _tile_spec(width=D):
    return pl.BlockSpec((ROW_TILE, width), lambda i: (i, 0))


def _const_spec(shape):
    nd = len(shape)
    return pl.BlockSpec(shape, lambda *_: (0,) * nd)


def _layer_spec(layer, shape):
    nd = len(shape)
    return pl.BlockSpec((None,) + tuple(shape), lambda *_: (layer,) + (0,) * nd)


ADA_NB = 512


def _ada_kernel(c_ref, w_ref, b_ref, o_ref):
    o_ref[...] = _dot3(_silu(c_ref[...]), w_ref[...]) + b_ref[...]


def _ada_call(cond, ada_w, ada_b):
    return pl.pallas_call(
        _ada_kernel,
        out_shape=jax.ShapeDtypeStruct((DEPTH, COND_ROWS, N_ADA * D), F32),
        grid=(DEPTH, N_ADA * D // ADA_NB),
        in_specs=[
            pl.BlockSpec((COND_ROWS, D), lambda l, n: (0, 0)),
            pl.BlockSpec((None, D, ADA_NB), lambda l, n: (l, 0, n)),
            pl.BlockSpec((None, 1, ADA_NB), lambda l, n: (l, 0, n)),
        ],
        out_specs=pl.BlockSpec((None, COND_ROWS, ADA_NB), lambda l, n: (l, 0, n)),
        compiler_params=_params(("arbitrary", "arbitrary")),
        name="ada_params",
    )(cond, ada_w, ada_b.reshape(DEPTH, 1, N_ADA * D))


def _conv_in_kernel(x_ref, ada_ref, g_ref, w1_ref, b1_ref, u_ref, w1b_ref):
    @pl.when(pl.program_id(0) == 0)
    def _():
        w1b_ref[...] = w1_ref[...].astype(BF16)

    h = _modulate(x_ref[...], g_ref[...], ada_ref[0:1, :], ada_ref[1:2, :])
    u = jnp.dot(h.astype(BF16), w1b_ref[...], preferred_element_type=F32) + b1_ref[...]
    u_ref[...] = u[:, :D] * jax.nn.sigmoid(u[:, D:])


def _conv_in_call(y, ada, norm_mix, conv_w1, conv_b1, layer, j):
    return pl.pallas_call(
        _conv_in_kernel,
        out_shape=jax.ShapeDtypeStruct((T, D), F32),
        grid=(N_TILES,),
        in_specs=[
            _tile_spec(),
            _ada_spec(layer),
            _layer_spec(layer, (1, D)),
            _layer_spec(j, (D, 2 * D)),
            _layer_spec(j, (1, 2 * D)),
        ],
        out_specs=_tile_spec(),
        scratch_shapes=[pltpu.VMEM((D, 2 * D), BF16)],
        compiler_params=_params(("arbitrary",)),
        name="conv_in",
    )(y, ada, norm_mix.reshape(DEPTH, 1, D), conv_w1, conv_b1.reshape(-1, 1, 2 * D))


CONV_ROWS = 32
CONV_COLS = 512
BUF_ROWS = ROW_TILE + 2 * HALO


def _conv_out_kernel(u_ref, up_ref, un_ref, y_ref, ada_ref, dw_ref, dwb_ref, gn_ref,
                     w2_ref, b2_ref, o_ref, buf_ref, z_ref, w2b_ref):
    i = pl.program_id(0)

    @pl.when(i == 0)
    def _():
        w2b_ref[...] = w2_ref[...].astype(BF16)

    lat = i >= N_CTX_TILES
    pos = jnp.where(lat, i - N_CTX_TILES, 0) % LAT_TILES_PER_SEQ
    has_prev = lat & (pos != 0)
    has_next = lat & (pos != LAT_TILES_PER_SEQ - 1)
    buf_ref[0:HALO, :] = jnp.where(has_prev, up_ref[...], 0.0)
    buf_ref[HALO:HALO + ROW_TILE, :] = u_ref[...]
    buf_ref[HALO + ROW_TILE:BUF_ROWS, :] = jnp.where(has_next, un_ref[...], 0.0)

    off = HALO - CONV_PAD
    for r0 in range(0, ROW_TILE, CONV_ROWS):
        for c0 in range(0, D, CONV_COLS):
            acc = jnp.zeros((CONV_ROWS, CONV_COLS), F32)
            for k in range(CONV_WIDTH):
                win = buf_ref[r0 + k + off:r0 + k + off + CONV_ROWS, c0:c0 + CONV_COLS]
                acc = acc + dw_ref[k:k + 1, c0:c0 + CONV_COLS] * win
            z_ref[r0:r0 + CONV_ROWS, c0:c0 + CONV_COLS] = acc + dwb_ref[:, c0:c0 + CONV_COLS]

    z = _silu(_rms(z_ref[...], gn_ref[...]))
    m = jnp.dot(z.astype(BF16), w2b_ref[...], preferred_element_type=F32) + b2_ref[...]
    o_ref[...] = y_ref[...] + ada_ref[2:3, :] * m


def _conv_out_call(u, y, ada, conv_dw, conv_dw_b, conv_norm, conv_w2, conv_b2, layer, j):
    halos_per_tile = ROW_TILE // HALO
    last_halo = T // HALO - 1
    return pl.pallas_call(
        _conv_out_kernel,
        out_shape=jax.ShapeDtypeStruct((T, D), F32),
        grid=(N_TILES,),
        in_specs=[
            _tile_spec(),
            pl.BlockSpec((HALO, D), lambda i: (jnp.maximum(i * halos_per_tile - 1, 0), 0)),
            pl.BlockSpec((HALO, D), lambda i: (jnp.minimum((i + 1) * halos_per_tile, last_halo), 0)),
            _tile_spec(),
            _ada_spec(layer),
            _layer_spec(j, (CONV_WIDTH + 1, D)),
            _layer_spec(j, (1, D)),
            _layer_spec(j, (1, D)),
            _layer_spec(j, (D, D)),
            _layer_spec(j, (1, D)),
        ],
        out_specs=_tile_spec(),
        scratch_shapes=[pltpu.VMEM((BUF_ROWS, D), F32), pltpu.VMEM((ROW_TILE, D), F32),
                        pltpu.VMEM((D, D), BF16)],
        compiler_params=_params(("arbitrary",)),
        name="conv_out",
    )(u, u, u, y, ada, jnp.pad(conv_dw, ((0, 0), (0, 1), (0, 0))), conv_dw_b.reshape(-1, 1, D),
      conv_norm.reshape(-1, 1, D), conv_w2, conv_b2.reshape(-1, 1, D))


def _swap_pairs(x, width):
    lane = lax.broadcasted_iota(I32, x.shape, 1)
    first = (lane % (2 * ROPE_PAIRS)) < ROPE_PAIRS
    return jnp.where(first, pltpu.roll(x, width - ROPE_PAIRS, 1), pltpu.roll(x, ROPE_PAIRS, 1))


def _qkv_kernel(x_ref, ada_ref, g_ref, w_ref, qn_ref, kn_ref, cos_ref, sin_ref,
                q_ref, k_ref, v_ref, wb_ref, hs_ref):
    i = pl.program_id(0)

    @pl.when(i == 0)
    def _():
        wb_ref[...] = w_ref[...].astype(BF16)
        r = lax.broadcasted_iota(I32, (D, D), 0) // HEAD_DIM
        c = lax.broadcasted_iota(I32, (D, D), 1) // HEAD_DIM
        hs_ref[...] = jnp.where(r == c, 1.0 / HEAD_DIM, 0.0).astype(BF16)

    h = _modulate(x_ref[...], g_ref[...], ada_ref[0:1, :], ada_ref[1:2, :])
    qkv = jnp.dot(h.astype(BF16), wb_ref[...], preferred_element_type=F32)
    q = qkv[:, :D]
    k = qkv[:, D:D + KV_WIDTH]
    v_ref[...] = qkv[:, D + KV_WIDTH:]
    q_ms = _dot3_lhs(q * q, hs_ref[...])
    k_ms = _dot3_lhs(k * k, hs_ref[0:KV_WIDTH, 0:KV_WIDTH])
    qn = q * lax.rsqrt(q_ms + NORM_EPS) * qn_ref[...]
    kn = k * lax.rsqrt(k_ms + NORM_EPS) * kn_ref[...]

    @pl.when(i < N_CTX_TILES)
    def _():
        q_ref[...] = qn
        k_ref[...] = kn

    @pl.when(i >= N_CTX_TILES)
    def _():
        cos = cos_ref[...]
        sin = sin_ref[...]
        q_ref[...] = qn * cos + _swap_pairs(qn, D) * sin
        k_ref[...] = kn * cos[:, :KV_WIDTH] + _swap_pairs(kn, KV_WIDTH) * sin[:, :KV_WIDTH]


def _dot3_lhs(a, b_bf16):
    a_hi, a_lo = _split(a)
    d = functools.partial(jnp.dot, preferred_element_type=F32)
    return d(a_hi, b_bf16) + d(a_lo, b_bf16)


def _rope_tables():
    pos = jnp.arange(LAT_LEN, dtype=I32)
    row = (pos // GRID_W).astype(F32)
    col = (pos % GRID_W).astype(F32)
    inv_freq = ROPE_THETA ** (-jnp.arange(ROPE_PAIRS, dtype=F32) / ROPE_PAIRS)
    d = jnp.arange(D, dtype=I32) % HEAD_DIM
    freq = inv_freq[d % ROPE_PAIRS]
    p = jnp.where((d >= HEAD_DIM // 2)[None, :], col[:, None], row[:, None])
    ang = p * freq[None, :]
    first = (d % (2 * ROPE_PAIRS)) < ROPE_PAIRS
    return jnp.cos(ang), jnp.where(first[None, :], -jnp.sin(ang), jnp.sin(ang))


def _qkv_call(y, ada, norm_mix, wqkv, q_norm, k_norm, layer, j):
    cos, sin = _rope_tables()
    table_spec = pl.BlockSpec(
        (ROW_TILE, D), lambda i: (jnp.maximum(i - N_CTX_TILES, 0) % LAT_TILES_PER_SEQ, 0))
    return pl.pallas_call(
        _qkv_kernel,
        out_shape=(jax.ShapeDtypeStruct((T, D), F32), jax.ShapeDtypeStruct((T, KV_WIDTH), F32),
                   jax.ShapeDtypeStruct((T, KV_WIDTH), F32)),
        grid=(N_TILES,),
        in_specs=[
            _tile_spec(),
            _ada_spec(layer),
            _layer_spec(layer, (1, D)),
            _layer_spec(j, (D, QKV_WIDTH)),
            _const_spec((1, D)),
            _const_spec((1, KV_WIDTH)),
            table_spec,
            table_spec,
        ],
        out_specs=(_tile_spec(), _tile_spec(KV_WIDTH), _tile_spec(KV_WIDTH)),
        scratch_shapes=[pltpu.VMEM((D, QKV_WIDTH), BF16), pltpu.VMEM((D, D), BF16)],
        compiler_params=_params(("arbitrary",)),
        name="qkv",
    )(y, ada, norm_mix.reshape(DEPTH, 1, D), wqkv,
      jnp.tile(q_norm[j], N_HEADS).reshape(1, D), jnp.tile(k_norm[j], N_KV).reshape(1, KV_WIDTH),
      cos, sin)


def _stack_heads(q, kv):
    return jnp.concatenate(
        [q[:, (kv * GQA + g) * HEAD_DIM:(kv * GQA + g + 1) * HEAD_DIM] for g in range(GQA)], axis=0)


def _sink_column(sink_ref, kv, rows):
    return jnp.concatenate(
        [jnp.broadcast_to(sink_ref[0:1, kv * GQA + g:kv * GQA + g + 1], (rows, 1)) for g in range(GQA)],
        axis=0)


def _qk(q, k):
    return lax.dot_general(q, k, (((1,), (1,)), ((), ())), preferred_element_type=F32) * (HEAD_DIM ** -0.5)


def _unstack_heads(o_groups, rows):
    return jnp.concatenate(
        [o[g * rows:(g + 1) * rows, :] for o in o_groups for g in range(GQA)], axis=1)


def _attn_ctx_kernel(q_ref, k_ref, v_ref, sink_ref, wo_ref, y_ref, ada_ref, o_ref, wob_ref):
    @pl.when(pl.program_id(0) == 0)
    def _():
        wob_ref[...] = wo_ref[...].astype(BF16)

    q = q_ref[...].astype(BF16)
    k = k_ref[...].astype(BF16)
    v = v_ref[...].astype(BF16)
    outs = []
    for kv in range(N_KV):
        sl = slice(kv * HEAD_DIM, (kv + 1) * HEAD_DIM)
        s = _qk(_stack_heads(q, kv), k[:, sl])
        sink = _sink_column(sink_ref, kv, CTX_LEN)
        m = jnp.maximum(jnp.max(s, axis=1, keepdims=True), sink)
        p = jnp.exp(s - m)
        den = jnp.sum(p, axis=1, keepdims=True) + jnp.exp(sink - m)
        outs.append(jnp.dot(p.astype(BF16), v[:, sl], preferred_element_type=F32) / den)
    o = _unstack_heads(outs, CTX_LEN)
    m_out = jnp.dot(o.astype(BF16), wob_ref[...], preferred_element_type=F32)
    o_ref[...] = y_ref[...] + ada_ref[2:3, :] * m_out


def _attn_ctx_call(q, k, v, sink, wo, y, ada, layer, j):
    return pl.pallas_call(
        _attn_ctx_kernel,
        out_shape=jax.ShapeDtypeStruct((T, D), F32),
        grid=(N_CTX_SEQ,),
        in_specs=[
            _tile_spec(), _tile_spec(KV_WIDTH), _tile_spec(KV_WIDTH),
            _layer_spec(j, (1, N_HEADS)),
            _layer_spec(j, (D, D)),
            _tile_spec(),
            _ada_spec(layer),
        ],
        out_specs=_tile_spec(),
        scratch_shapes=[pltpu.VMEM((D, D), BF16)],
        input_output_aliases={5: 0},
        compiler_params=_params(("arbitrary",)),
        name="attn_ctx",
    )(q, k, v, sink.reshape(-1, 1, N_HEADS), wo, y, ada)


LAT_WIN = 3 * WINDOW
N_QBLOCKS = LAT_LEN // Q_BLOCK


def _attn_lat_kernel(q_ref, k_ref, v_ref, kc_ref, vc_ref, sink_ref, wo_ref, y_ref, ada_ref,
                     o_ref, wob_ref):
    b = pl.program_id(0)
    n = pl.program_id(1)

    @pl.when((b == 0) & (n == 0))
    def _():
        wob_ref[...] = wo_ref[...].astype(BF16)

    start = pl.multiple_of(jnp.clip((n - 1) * WINDOW, 0, LAT_LEN - LAT_WIN), WINDOW)
    q = q_ref[...].astype(BF16)
    kw = k_ref[pl.ds(start, LAT_WIN), :].astype(BF16)
    vw = v_ref[pl.ds(start, LAT_WIN), :].astype(BF16)
    kc = kc_ref[...].astype(BF16)
    vc = vc_ref[...].astype(BF16)
    qpos = n * Q_BLOCK + lax.broadcasted_iota(I32, (GQA * Q_BLOCK, LAT_WIN), 0) % Q_BLOCK
    kpos = start + lax.broadcasted_iota(I32, (GQA * Q_BLOCK, LAT_WIN), 1)
    in_band = jnp.abs(kpos - qpos) <= WINDOW
    outs = []
    for kv in range(N_KV):
        sl = slice(kv * HEAD_DIM, (kv + 1) * HEAD_DIM)
        qg = _stack_heads(q, kv)
        s_loc = jnp.where(in_band, _qk(qg, kw[:, sl]), -jnp.inf)
        s_ctx = _qk(qg, kc[:, sl])
        sink = _sink_column(sink_ref, kv, Q_BLOCK)
        m = jnp.maximum(jnp.maximum(jnp.max(s_loc, axis=1, keepdims=True),
                                    jnp.max(s_ctx, axis=1, keepdims=True)), sink)
        p_loc = jnp.exp(s_loc - m)
        p_ctx = jnp.exp(s_ctx - m)
        den = (jnp.sum(p_loc, axis=1, keepdims=True) + jnp.sum(p_ctx, axis=1, keepdims=True)
               + jnp.exp(sink - m))
        o = (jnp.dot(p_loc.astype(BF16), vw[:, sl], preferred_element_type=F32)
             + jnp.dot(p_ctx.astype(BF16), vc[:, sl], preferred_element_type=F32))
        outs.append(o / den)
    o = _unstack_heads(outs, Q_BLOCK)
    m_out = jnp.dot(o.astype(BF16), wob_ref[...], preferred_element_type=F32)
    o_ref[...] = y_ref[...] + ada_ref[2:3, :] * m_out


def _attn_lat_call(q, k, v, cache_k, cache_v, sink, wo, y, ada, layer, j):
    q_row0 = TP // Q_BLOCK
    seq0 = TP // LAT_LEN
    qspec = pl.BlockSpec((Q_BLOCK, D), lambda b, n: (q_row0 + b * N_QBLOCKS + n, 0))
    kvspec = pl.BlockSpec((LAT_LEN, KV_WIDTH), lambda b, n: (seq0 + b, 0))
    cspec = pl.BlockSpec((None, None, CTX_LEN, KV_WIDTH), lambda b, n: (b, j, 0, 0))
    ck = cache_k.reshape(N_LAT_SEQ, -1, CTX_LEN, KV_WIDTH)
    cv = cache_v.reshape(N_LAT_SEQ, -1, CTX_LEN, KV_WIDTH)
    return pl.pallas_call(
        _attn_lat_kernel,
        out_shape=jax.ShapeDtypeStruct((T, D), F32),
        grid=(N_LAT_SEQ, N_QBLOCKS),
        in_specs=[
            qspec, kvspec, kvspec, cspec, cspec,
            _layer_spec(j, (1, N_HEADS)),
            _layer_spec(j, (D, D)),
            qspec,
            pl.BlockSpec((None, None, N_ADA, D), lambda b, n: (layer, 1 + b, 0, 0)),
        ],
        out_specs=qspec,
        scratch_shapes=[pltpu.VMEM((D, D), BF16)],
        input_output_aliases={7: 0},
        compiler_params=_params(("arbitrary", "arbitrary")),
        name="attn_lat",
    )(q, k, v, ck, cv, sink.reshape(-1, 1, N_HEADS), wo, y, ada)


def _first_index(mask, idx, sentinel):
    return jnp.min(jnp.where(mask, idx, sentinel), axis=1, keepdims=True)


def _router_kernel(y_ref, ada_ref, g_ref, rw_ref, rb_ref, sg_ref, su_ref, sd_ref,
                   h_ref, base_ref, key_ref, w_ref, cnt_ref,
                   carry_ref, tri_ref, sgb_ref, sub_ref, sdb_ref):
    i = pl.program_id(0)

    @pl.when(i == 0)
    def _():
        sgb_ref[...] = sg_ref[...].astype(BF16)
        sub_ref[...] = su_ref[...].astype(BF16)
        sdb_ref[...] = sd_ref[...].astype(BF16)
        r = lax.broadcasted_iota(I32, (ROW_TILE, ROW_TILE), 0)
        c = lax.broadcasted_iota(I32, (ROW_TILE, ROW_TILE), 1)
        tri_ref[...] = jnp.where(c < r, 1.0, 0.0).astype(BF16)

    @pl.when(i % TILES_PER_GROUP == 0)
    def _():
        carry_ref[...] = jnp.zeros_like(carry_ref)

    y = y_ref[...]
    h = _modulate(y, g_ref[...], ada_ref[3:4, :], ada_ref[4:5, :])
    for c in range(LANE_CHUNKS):
        h_ref[pl.ds(c, ROW_TILE, stride=LANE_CHUNKS), :] = h[:, c * LANES:(c + 1) * LANES]
    hb = h.astype(BF16)

    d = functools.partial(jnp.dot, preferred_element_type=F32)
    act = (_silu(d(hb, sgb_ref[...])) * d(hb, sub_ref[...])).astype(BF16)
    base_ref[...] = y + ada_ref[5:6, :] * d(act, sdb_ref[...])

    scores = jax.nn.sigmoid(_dot3(h, rw_ref[...]))
    choice = scores + rb_ref[...]
    shape = (ROW_TILE, N_EXPERTS)
    lane = lax.broadcasted_iota(I32, shape, 1)
    grp = lane // EGROUP
    neg = jnp.full(shape, -jnp.inf, F32)

    gscore = neg
    for g in range(N_EGROUPS):
        in_g = grp == g
        mg = jnp.where(in_g, choice, neg)
        m1 = jnp.max(mg, axis=1, keepdims=True)
        i1 = _first_index(mg == m1, lane, N_EXPERTS)
        m2 = jnp.max(jnp.where(lane == i1, neg, mg), axis=1, keepdims=True)
        gscore = jnp.where(in_g, m1 + m2, gscore)
    allowed = jnp.zeros(shape, jnp.bool_)
    for _ in range(TOPK_GROUPS):
        gm = jnp.max(gscore, axis=1, keepdims=True)
        gi = _first_index(gscore == gm, grp, N_EGROUPS)
        hit = grp == gi
        allowed = allowed | hit
        gscore = jnp.where(hit, neg, gscore)

    cm = jnp.where(allowed, choice, neg)
    picked = jnp.zeros(shape, F32)
    e_cols, s_cols = [], []
    for _ in range(TOP_K):
        m = jnp.max(cm, axis=1, keepdims=True)
        ik = _first_index(cm == m, lane, N_EXPERTS)
        hit = lane == ik
        e_cols.append(ik)
        s_cols.append(jnp.sum(jnp.where(hit, scores, 0.0), axis=1, keepdims=True))
        cm = jnp.where(hit, neg, cm)
        picked = jnp.where(hit, 1.0, picked)

    cum = d(tri_ref[...], picked.astype(BF16)) + carry_ref[...]
    carry_ref[...] = carry_ref[...] + jnp.sum(picked, axis=0, keepdims=True)
    cnt_ref[...] = carry_ref[...]

    denom = functools.reduce(lambda a, b: a + b, s_cols)
    kcol = lax.broadcasted_iota(I32, (ROW_TILE, TOP_K), 1)
    keys = jnp.zeros((ROW_TILE, TOP_K), I32)
    ws = jnp.zeros((ROW_TILE, TOP_K), F32)
    for kk in range(TOP_K):
        rank = jnp.sum(jnp.where(lane == e_cols[kk], cum, 0.0), axis=1, keepdims=True).astype(I32)
        keys = jnp.where(kcol == kk, (e_cols[kk] << RANK_BITS) | rank, keys)
        ws = jnp.where(kcol == kk, s_cols[kk] / denom * ROUTED_SCALE, ws)
    key_ref[...] = keys
    w_ref[...] = ws


def _router_call(y, ada, norm_ffn, router_w, router_bias, sg, su, sd, layer):
    return pl.pallas_call(
        _router_kernel,
        out_shape=(
            jax.ShapeDtypeStruct((T * LANE_CHUNKS, LANES), F32),
            jax.ShapeDtypeStruct((T, D), F32),
            jax.ShapeDtypeStruct((T, TOP_K), I32),
            jax.ShapeDtypeStruct((T, TOP_K), F32),
            jax.ShapeDtypeStruct((N_TILES, 1, N_EXPERTS), F32),
        ),
        grid=(N_TILES,),
        in_specs=[
            _tile_spec(),
            _ada_spec(layer),
            _layer_spec(layer, (1, D)),
            _layer_spec(layer, (D, N_EXPERTS)),
            _layer_spec(layer, (1, N_EXPERTS)),
            _layer_spec(layer, (D, D_SHARED)),
            _layer_spec(layer, (D, D_SHARED)),
            _layer_spec(layer, (D_SHARED, D)),
        ],
        out_specs=(
            pl.BlockSpec((ROW_TILE * LANE_CHUNKS, LANES), lambda i: (i, 0)),
            _tile_spec(),
            _tile_spec(TOP_K),
            _tile_spec(TOP_K),
            pl.BlockSpec((None, 1, N_EXPERTS), lambda i: (i, 0, 0)),
        ),
        scratch_shapes=[
            pltpu.VMEM((1, N_EXPERTS), F32),
            pltpu.VMEM((ROW_TILE, ROW_TILE), BF16),
            pltpu.VMEM((D, D_SHARED), BF16),
            pltpu.VMEM((D, D_SHARED), BF16),
            pltpu.VMEM((D_SHARED, D), BF16),
        ],
        compiler_params=_params(("arbitrary",)),
        name="router",
    )(y, ada, norm_ffn.reshape(DEPTH, 1, D), router_w, router_bias.reshape(DEPTH, 1, N_EXPERTS),
      sg, su, sd)


SLOT_ROWS = MOE_SLOTS // LANES
ASSIGN_ROWS = N_ASSIGN // LANES
LANE_BITS = 7


def _slot_kernel(key_ref, start_ref, fill_lo_ref, fill_hi_ref, slot_ref):
    def place(a, carry):
        key = key_ref[a >> LANE_BITS, a & (LANES - 1)]
        p = start_ref[key >> RANK_BITS] + (key & ((1 << RANK_BITS) - 1))
        slot_ref[p >> LANE_BITS, p & (LANES - 1)] = a
        return carry

    lax.fori_loop(0, N_ASSIGN, place, 0, unroll=8)

    def fill_range(e, carry):
        def fill(p, c):
            slot_ref[p >> LANE_BITS, p & (LANES - 1)] = DUMMY_ASSIGN
            return c

        lax.fori_loop(fill_lo_ref[e], fill_hi_ref[e], fill, 0)
        return carry

    lax.fori_loop(0, N_EXPERTS, fill_range, 0)


def _slot_call(keys, starts, fill_lo, fill_hi):
    smem = pl.BlockSpec(memory_space=pltpu.SMEM)
    return pl.pallas_call(
        _slot_kernel,
        out_shape=jax.ShapeDtypeStruct((SLOT_ROWS, LANES), I32),
        in_specs=[smem, smem, smem, smem],
        out_specs=smem,
        name="moe_slots",
    )(keys, starts, fill_lo, fill_hi)


W_ROWS = ASSIGN_ROWS + SUBLANES


def _expert_kernel(blk_exp_ref, n_used_ref,
                   slot_hbm, wflat_hbm, x_hbm, wg_ref, wu_ref, wd_ref,
                   out_hbm,
                   x_vmem, acc_vmem, tile_ref, wgb_ref, wub_ref, wdb_ref, slot_smem, w_smem, sems):
    g = pl.program_id(0)
    j = pl.program_id(1)
    idx = g * MOE_NB + j

    def group_copies():
        return (pltpu.make_async_copy(x_hbm.at[g], x_vmem, sems.at[0]),
                pltpu.make_async_copy(slot_hbm.at[g], slot_smem, sems.at[1]),
                pltpu.make_async_copy(wflat_hbm.at[g], w_smem, sems.at[2]))

    @pl.when(j == 0)
    def _():
        for c in group_copies():
            c.start()
        acc_vmem[...] = jnp.zeros_like(acc_vmem)
        for c in group_copies():
            c.wait()

    e_cur = blk_exp_ref[idx]
    e_prev = blk_exp_ref[jnp.maximum(idx - 1, 0)]

    @pl.when((j == 0) | (e_cur != e_prev))
    def _():
        wgb_ref[...] = wg_ref[...].astype(BF16)
        wub_ref[...] = wu_ref[...].astype(BF16)
        wdb_ref[...] = wd_ref[...].astype(BF16)

    @pl.when(j < n_used_ref[g])
    def _():
        rows_per_block = MOE_BM // LANES

        def assignment(r):
            return slot_smem[j * rows_per_block + r // LANES, r % LANES]

        for r in range(MOE_BM):
            t = jnp.minimum(assignment(r) >> 3, MOE_GROUP - 1)
            tile_ref[pl.ds(r * SUBLANES, SUBLANES), :] = x_vmem[t]
        xs = jnp.concatenate(
            [tile_ref[pl.ds(c, MOE_BM, stride=LANE_CHUNKS), :] for c in range(LANE_CHUNKS)],
            axis=1).astype(BF16)
        d = functools.partial(jnp.dot, preferred_element_type=F32)
        act = (_silu(d(xs, wgb_ref[...])) * d(xs, wub_ref[...])).astype(BF16)
        yb = d(act, wdb_ref[...])
        for c in range(LANE_CHUNKS):
            tile_ref[pl.ds(c, MOE_BM, stride=LANE_CHUNKS), :] = yb[:, c * LANES:(c + 1) * LANES]
        for r in range(MOE_BM):
            a = assignment(r)
            t = a >> 3
            w = w_smem[a >> LANE_BITS, a & (LANES - 1)]
            acc_vmem[t] = acc_vmem[t] + w * tile_ref[pl.ds(r * SUBLANES, SUBLANES), :]

    @pl.when(j == MOE_NB - 1)
    def _():
        out_copy = pltpu.make_async_copy(acc_vmem, out_hbm.at[g], sems.at[3])
        out_copy.start()
        out_copy.wait()


def _expert_call(blk_exp, n_used, slots, wflat, x3, wg, wu, wd, layer):
    def wspec(shape):
        return pl.BlockSpec((None, None) + shape,
                            lambda g, j, be, nu: (layer, be[g * MOE_NB + j], 0, 0))

    any_spec = pl.BlockSpec(memory_space=pl.ANY)
    return pl.pallas_call(
        _expert_kernel,
        out_shape=jax.ShapeDtypeStruct((N_GROUPS, ACC_ROWS, SUBLANES, LANES), F32),
        grid_spec=pltpu.PrefetchScalarGridSpec(
            num_scalar_prefetch=2,
            grid=(N_GROUPS, MOE_NB),
            in_specs=[any_spec, any_spec, any_spec,
                      wspec((D, D_EXPERT)), wspec((D, D_EXPERT)), wspec((D_EXPERT, D))],
            out_specs=any_spec,
            scratch_shapes=[
                pltpu.VMEM((MOE_GROUP, SUBLANES, LANES), F32),
                pltpu.VMEM((ACC_ROWS, SUBLANES, LANES), F32),
                pltpu.VMEM((MOE_BM * SUBLANES, LANES), F32),
                pltpu.VMEM((D, D_EXPERT), BF16),
                pltpu.VMEM((D, D_EXPERT), BF16),
                pltpu.VMEM((D_EXPERT, D), BF16),
                pltpu.SMEM((SLOT_ROWS, LANES), I32),
                pltpu.SMEM((W_ROWS, LANES), F32),
                pltpu.SemaphoreType.DMA((4,)),
            ]),
        compiler_params=_params(("arbitrary", "arbitrary")),
        name="moe_experts",
    )(blk_exp, n_used, slots, wflat, x3, wg, wu, wd)


def _finish_kernel(base_ref, routed_ref, ada_ref, o_ref):
    routed = jnp.concatenate(
        [routed_ref[pl.ds(c, ROW_TILE, stride=LANE_CHUNKS), :] for c in range(LANE_CHUNKS)], axis=1)
    o_ref[...] = base_ref[...] + ada_ref[5:6, :] * routed


def _finish_call(base, routed, ada, layer):
    rows = ROW_TILE * LANE_CHUNKS
    return pl.pallas_call(
        _finish_kernel,
        out_shape=jax.ShapeDtypeStruct((T, D), F32),
        grid=(N_TILES,),
        in_specs=[
            _tile_spec(),
            pl.BlockSpec((None, rows, LANES), lambda i: (i // TILES_PER_GROUP, i % TILES_PER_GROUP, 0)),
            _ada_spec(layer),
        ],
        out_specs=_tile_spec(),
        compiler_params=_params(("arbitrary",)),
        name="moe_finish",
    )(base, routed.reshape(N_GROUPS, ACC_ROWS * SUBLANES, LANES), ada)


def _moe_layer(y, ada, norm_ffn, router_w, router_bias, wg, wu, wd, sg, su, sd, layer):
    h_tiles, base, keys, ws, cnts = _router_call(
        y, ada, norm_ffn, router_w, router_bias, sg, su, sd, layer)

    counts = cnts.reshape(N_GROUPS, TILES_PER_GROUP, N_EXPERTS)[:, -1, :].astype(I32)
    padded = (counts + MOE_BM - 1) // MOE_BM * MOE_BM
    pad_end = jnp.cumsum(padded, axis=1)
    starts = pad_end - padded
    n_used = (pad_end[:, -1] // MOE_BM).astype(I32)
    blocks = jnp.arange(MOE_NB, dtype=I32) * MOE_BM
    blk_exp = jnp.minimum(
        jnp.sum(blocks[None, :, None] >= pad_end[:, None, :], axis=2), N_EXPERTS - 1).astype(I32)
    fill_lo = starts + counts
    fill_hi = pad_end.at[:, -1].set(MOE_SLOTS)

    keys_g = keys.reshape(N_GROUPS, ASSIGN_ROWS, LANES)
    slots = jnp.stack([_slot_call(keys_g[g], starts[g], fill_lo[g], fill_hi[g])
                       for g in range(N_GROUPS)])
    wflat = jnp.pad(ws.reshape(N_GROUPS, ASSIGN_ROWS, LANES), ((0, 0), (0, SUBLANES), (0, 0)))
    x3 = h_tiles.reshape(N_GROUPS, MOE_GROUP, SUBLANES, LANES)
    routed = _expert_call(blk_exp.reshape(-1), n_used, slots, wflat, x3, wg, wu, wd, layer)
    return _finish_call(base, routed, ada, layer)


def kernel(x_prompt, x_sample, cache_k, cache_v, c, c_ctx, ada_w, ada_b, norm_mix, norm_ffn,
           conv_w1, conv_b1, conv_dw, conv_dw_b, conv_norm, conv_w2, conv_b2,
           attn_wqkv, attn_q_norm, attn_k_norm, attn_sink, attn_wo,
           router_w, router_bias, exp_w_gate, exp_w_up, exp_w_down,
           sh_w_gate, sh_w_up, sh_w_down):
    y = jnp.concatenate([x_prompt.reshape(TP, D), x_sample.reshape(TS, D)], axis=0)
    cond = jnp.concatenate(
        [c_ctx[None, :], c, jnp.zeros((COND_ROWS - N_COND, D), F32)], axis=0)
    ada = _ada_call(cond, ada_w, ada_b).reshape(DEPTH, COND_ROWS, N_ADA, D)

    new_k = new_v = None
    for layer in range(DEPTH):
        j = layer // 2
        if layer % 2 == 0:
            u = _conv_in_call(y, ada, norm_mix, conv_w1, conv_b1, layer, j)
            y = _conv_out_call(u, y, ada, conv_dw, conv_dw_b, conv_norm, conv_w2, conv_b2, layer, j)
        else:
            q, k, v = _qkv_call(y, ada, norm_mix, attn_wqkv, attn_q_norm, attn_k_norm, layer, j)
            new_k = k[:TP].reshape(N_CTX_SEQ, 1, CTX_LEN, N_KV, HEAD_DIM)
            new_v = v[:TP].reshape(N_CTX_SEQ, 1, CTX_LEN, N_KV, HEAD_DIM)
            y = _attn_ctx_call(q, k, v, attn_sink, attn_wo, y, ada, layer, j)
            y = _attn_lat_call(q, k, v, cache_k, cache_v, attn_sink, attn_wo, y, ada, layer, j)
        y = _moe_layer(y, ada, norm_ffn, router_w, router_bias, exp_w_gate, exp_w_up, exp_w_down,
                       sh_w_gate, sh_w_up, sh_w_down, layer)

    y_p = y[:TP].reshape(N_CTX_SEQ, CTX_LEN, D)
    y_s = y[TP:].reshape(N_LAT_SEQ, LAT_LEN, D)
    return (y_p, y_s, new_k, new_v)
```

```python
import functools

import jax
import jax.numpy as jnp
from jax import lax
from jax.experimental import pallas as pl
from jax.experimental.pallas import tpu as pltpu

F32 = jnp.float32
BF16 = jnp.bfloat16
I32 = jnp.int32

D = 1024
N_CTX_SEQ = 32
CTX_LEN = 256
N_LAT_SEQ = 4
LAT_LEN = 1024
TP = N_CTX_SEQ * CTX_LEN
TS = N_LAT_SEQ * LAT_LEN
T = TP + TS
DEPTH = 2
N_ADA = 6
N_COND = 1 + N_LAT_SEQ
COND_ROWS = 8
CONV_WIDTH = 31
CONV_PAD = CONV_WIDTH // 2
N_HEADS = 16
N_KV = 4
HEAD_DIM = 64
GQA = N_HEADS // N_KV
KV_WIDTH = N_KV * HEAD_DIM
QKV_WIDTH = D + 2 * KV_WIDTH
WINDOW = 128
GRID_W = 64
ROPE_PAIRS = HEAD_DIM // 4
ROPE_THETA = 10000.0
N_EXPERTS = 64
N_EGROUPS = 8
EGROUP = N_EXPERTS // N_EGROUPS
TOPK_GROUPS = 4
TOP_K = 8
D_EXPERT = 256
D_SHARED = 256
ROUTED_SCALE = 2.5
NORM_EPS = 1e-6

LANES = 128
SUBLANES = 8
VMEM_LIMIT = 56 * 1024 * 1024

ROW_TILE = 256
N_TILES = T // ROW_TILE
N_CTX_TILES = TP // ROW_TILE
LAT_TILES_PER_SEQ = LAT_LEN // ROW_TILE
HALO = 16
LANE_CHUNKS = D // LANES

MOE_GROUP = 4096
N_GROUPS = T // MOE_GROUP
TILES_PER_GROUP = MOE_GROUP // ROW_TILE
MOE_BM = 256
MOE_NB = MOE_GROUP * TOP_K // MOE_BM + N_EXPERTS
MOE_SLOTS = MOE_NB * MOE_BM
N_ASSIGN = MOE_GROUP * TOP_K
DUMMY_ASSIGN = N_ASSIGN
ACC_ROWS = MOE_GROUP + SUBLANES
RANK_BITS = 12
Q_BLOCK = 128


def _cond_of_tile(i):
    return jnp.where(i < N_CTX_TILES, 0, 1 + (i - N_CTX_TILES) // LAT_TILES_PER_SEQ)


def _params(sem, vmem=VMEM_LIMIT):
    return pltpu.CompilerParams(dimension_semantics=sem, vmem_limit_bytes=vmem)


def _bdot(a, b):
    return jnp.dot(a.astype(BF16), b.astype(BF16), preferred_element_type=F32)


def _split(a):
    hi = a.astype(BF16)
    lo = (a - hi.astype(F32)).astype(BF16)
    return hi, lo


def _dot3(a, b):
    a_hi, a_lo = _split(a)
    b_hi, b_lo = _split(b)
    d = functools.partial(jnp.dot, preferred_element_type=F32)
    return d(a_hi, b_hi) + d(a_lo, b_hi) + d(a_hi, b_lo)


def _rms(x, g):
    return x * lax.rsqrt(jnp.mean(x * x, axis=-1, keepdims=True) + NORM_EPS) * g


def _modulate(x, g, shift, scale):
    return _rms(x, g) * (1.0 + scale) + shift


def _silu(x):
    return x * jax.nn.sigmoid(x)


def _ada_spec(layer):
    return pl.BlockSpec((None, None, N_ADA, D), lambda i: (layer, _cond_of_tile(i), 0, 0))


def _tile_spec(width=D):
    return pl.BlockSpec((ROW_TILE, width), lambda i: (i, 0))


def _const_spec(shape):
    nd = len(shape)
    return pl.BlockSpec(shape, lambda *_: (0,) * nd)


def _layer_spec(layer, shape):
    nd = len(shape)
    return pl.BlockSpec((None,) + tuple(shape), lambda *_: (layer,) + (0,) * nd)


ADA_NB = 512


def _ada_kernel(c_ref, w_ref, b_ref, o_ref):
    o_ref[...] = _dot3(_silu(c_ref[...]), w_ref[...]) + b_ref[...]


def _ada_call(cond, ada_w, ada_b):
    return pl.pallas_call(
        _ada_kernel,
        out_shape=jax.ShapeDtypeStruct((DEPTH, COND_ROWS, N_ADA * D), F32),
        grid=(DEPTH, N_ADA * D // ADA_NB),
        in_specs=[
            pl.BlockSpec((COND_ROWS, D), lambda l, n: (0, 0)),
            pl.BlockSpec((None, D, ADA_NB), lambda l, n: (l, 0, n)),
            pl.BlockSpec((None, 1, ADA_NB), lambda l, n: (l, 0, n)),
        ],
        out_specs=pl.BlockSpec((None, COND_ROWS, ADA_NB), lambda l, n: (l, 0, n)),
        compiler_params=_params(("arbitrary", "arbitrary")),
        name="ada_params",
    )(cond, ada_w, ada_b.reshape(DEPTH, 1, N_ADA * D))


def _conv_in_kernel(x_ref, ada_ref, g_ref, w1_ref, b1_ref, u_ref, w1b_ref):
    @pl.when(pl.program_id(0) == 0)
    def _():
        w1b_ref[...] = w1_ref[...].astype(BF16)

    h = _modulate(x_ref[...], g_ref[...], ada_ref[0:1, :], ada_ref[1:2, :])
    u = jnp.dot(h.astype(BF16), w1b_ref[...], preferred_element_type=F32) + b1_ref[...]
    u_ref[...] = u[:, :D] * jax.nn.sigmoid(u[:, D:])


def _conv_in_call(y, ada, norm_mix, conv_w1, conv_b1, layer, j):
    return pl.pallas_call(
        _conv_in_kernel,
        out_shape=jax.ShapeDtypeStruct((T, D), F32),
        grid=(N_TILES,),
        in_specs=[
            _tile_spec(),
            _ada_spec(layer),
            _layer_spec(layer, (1, D)),
            _layer_spec(j, (D, 2 * D)),
            _layer_spec(j, (1, 2 * D)),
        ],
        out_specs=_tile_spec(),
        scratch_shapes=[pltpu.VMEM((D, 2 * D), BF16)],
        compiler_params=_params(("arbitrary",)),
        name="conv_in",
    )(y, ada, norm_mix.reshape(DEPTH, 1, D), conv_w1, conv_b1.reshape(-1, 1, 2 * D))


CONV_ROWS = 64
BUF_ROWS = ROW_TILE + 2 * HALO


def _conv_out_kernel(u_ref, up_ref, un_ref, y_ref, ada_ref, dw_ref, dwb_ref, gn_ref,
                     w2_ref, b2_ref, o_ref, buf_ref, z_ref, w2b_ref):
    i = pl.program_id(0)

    @pl.when(i == 0)
    def _():
        w2b_ref[...] = w2_ref[...].astype(BF16)

    lat = i >= N_CTX_TILES
    pos = jnp.where(lat, i - N_CTX_TILES, 0) % LAT_TILES_PER_SEQ
    has_prev = lat & (pos != 0)
    has_next = lat & (pos != LAT_TILES_PER_SEQ - 1)
    for c in range(LANE_CHUNKS):
        cs = slice(c * LANES, (c + 1) * LANES)
        buf_ref[c, 0:HALO, :] = jnp.where(has_prev, up_ref[:, cs], 0.0)
        buf_ref[c, HALO:HALO + ROW_TILE, :] = u_ref[:, cs]
        buf_ref[c, HALO + ROW_TILE:BUF_ROWS, :] = jnp.where(has_next, un_ref[:, cs], 0.0)

    off = HALO - CONV_PAD
    for c in range(LANE_CHUNKS):
        cs = slice(c * LANES, (c + 1) * LANES)
        for r0 in range(0, ROW_TILE, CONV_ROWS):
            acc = jnp.broadcast_to(dwb_ref[:, cs], (CONV_ROWS, LANES))
            for k in range(CONV_WIDTH):
                win = buf_ref[c, r0 + k + off:r0 + k + off + CONV_ROWS, :]
                acc = acc + dw_ref[k:k + 1, cs] * win
            z_ref[r0:r0 + CONV_ROWS, cs] = acc

    z = _silu(_rms(z_ref[...], gn_ref[...]))
    m = jnp.dot(z.astype(BF16), w2b_ref[...], preferred_element_type=F32) + b2_ref[...]
    o_ref[...] = y_ref[...] + ada_ref[2:3, :] * m


def _conv_out_call(u, y, ada, conv_dw, conv_dw_b, conv_norm, conv_w2, conv_b2, layer, j):
    halos_per_tile = ROW_TILE // HALO
    last_halo = T // HALO - 1
    return pl.pallas_call(
        _conv_out_kernel,
        out_shape=jax.ShapeDtypeStruct((T, D), F32),
        grid=(N_TILES,),
        in_specs=[
            _tile_spec(),
            pl.BlockSpec((HALO, D), lambda i: (jnp.maximum(i * halos_per_tile - 1, 0), 0)),
            pl.BlockSpec((HALO, D), lambda i: (jnp.minimum((i + 1) * halos_per_tile, last_halo), 0)),
            _tile_spec(),
            _ada_spec(layer),
            _layer_spec(j, (CONV_WIDTH + 1, D)),
            _layer_spec(j, (1, D)),
            _layer_spec(j, (1, D)),
            _layer_spec(j, (D, D)),
            _layer_spec(j, (1, D)),
        ],
        out_specs=_tile_spec(),
        scratch_shapes=[pltpu.VMEM((LANE_CHUNKS, BUF_ROWS, LANES), F32), pltpu.VMEM((ROW_TILE, D), F32),
                        pltpu.VMEM((D, D), BF16)],
        compiler_params=_params(("arbitrary",)),
        name="conv_out",
    )(u, u, u, y, ada, jnp.pad(conv_dw, ((0, 0), (0, 1), (0, 0))), conv_dw_b.reshape(-1, 1, D),
      conv_norm.reshape(-1, 1, D), conv_w2, conv_b2.reshape(-1, 1, D))


def _swap_pairs(x, width):
    lane = lax.broadcasted_iota(I32, x.shape, 1)
    first = (lane % (2 * ROPE_PAIRS)) < ROPE_PAIRS
    return jnp.where(first, pltpu.roll(x, width - ROPE_PAIRS, 1), pltpu.roll(x, ROPE_PAIRS, 1))


def _qkv_kernel(x_ref, ada_ref, g_ref, w_ref, qn_ref, kn_ref, cos_ref, sin_ref,
                q_ref, k_ref, v_ref, wb_ref, hs_ref):
    i = pl.program_id(0)

    @pl.when(i == 0)
    def _():
        wb_ref[...] = w_ref[...].astype(BF16)
        r = lax.broadcasted_iota(I32, (D, D), 0) // HEAD_DIM
        c = lax.broadcasted_iota(I32, (D, D), 1) // HEAD_DIM
        hs_ref[...] = jnp.where(r == c, 1.0 / HEAD_DIM, 0.0).astype(BF16)

    h = _modulate(x_ref[...], g_ref[...], ada_ref[0:1, :], ada_ref[1:2, :])
    qkv = jnp.dot(h.astype(BF16), wb_ref[...], preferred_element_type=F32)
    q = qkv[:, :D]
    k = qkv[:, D:D + KV_WIDTH]
    v_ref[...] = qkv[:, D + KV_WIDTH:]
    q_ms = _dot3_lhs(q * q, hs_ref[...])
    k_ms = _dot3_lhs(k * k, hs_ref[0:KV_WIDTH, 0:KV_WIDTH])
    qn = q * lax.rsqrt(q_ms + NORM_EPS) * qn_ref[...]
    kn = k * lax.rsqrt(k_ms + NORM_EPS) * kn_ref[...]

    @pl.when(i < N_CTX_TILES)
    def _():
        q_ref[...] = qn
        k_ref[...] = kn

    @pl.when(i >= N_CTX_TILES)
    def _():
        cos = cos_ref[...]
        sin = sin_ref[...]
        q_ref[...] = qn * cos + _swap_pairs(qn, D) * sin
        k_ref[...] = kn * cos[:, :KV_WIDTH] + _swap_pairs(kn, KV_WIDTH) * sin[:, :KV_WIDTH]


def _dot3_lhs(a, b_bf16):
    a_hi, a_lo = _split(a)
    d = functools.partial(jnp.dot, preferred_element_type=F32)
    return d(a_hi, b_bf16) + d(a_lo, b_bf16)


def _rope_tables():
    pos = jnp.arange(LAT_LEN, dtype=I32)
    row = (pos // GRID_W).astype(F32)
    col = (pos % GRID_W).astype(F32)
    inv_freq = ROPE_THETA ** (-jnp.arange(ROPE_PAIRS, dtype=F32) / ROPE_PAIRS)
    d = jnp.arange(D, dtype=I32) % HEAD_DIM
    freq = inv_freq[d % ROPE_PAIRS]
    p = jnp.where((d >= HEAD_DIM // 2)[None, :], col[:, None], row[:, None])
    ang = p * freq[None, :]
    first = (d % (2 * ROPE_PAIRS)) < ROPE_PAIRS
    return jnp.cos(ang), jnp.where(first[None, :], -jnp.sin(ang), jnp.sin(ang))


def _qkv_call(y, ada, norm_mix, wqkv, q_norm, k_norm, layer, j):
    cos, sin = _rope_tables()
    table_spec = pl.BlockSpec(
        (ROW_TILE, D), lambda i: (jnp.maximum(i - N_CTX_TILES, 0) % LAT_TILES_PER_SEQ, 0))
    return pl.pallas_call(
        _qkv_kernel,
        out_shape=(jax.ShapeDtypeStruct((T, D), F32), jax.ShapeDtypeStruct((T, KV_WIDTH), F32),
                   jax.ShapeDtypeStruct((T, KV_WIDTH), F32)),
        grid=(N_TILES,),
        in_specs=[
            _tile_spec(),
            _ada_spec(layer),
            _layer_spec(layer, (1, D)),
            _layer_spec(j, (D, QKV_WIDTH)),
            _const_spec((1, D)),
            _const_spec((1, KV_WIDTH)),
            table_spec,
            table_spec,
        ],
        out_specs=(_tile_spec(), _tile_spec(KV_WIDTH), _tile_spec(KV_WIDTH)),
        scratch_shapes=[pltpu.VMEM((D, QKV_WIDTH), BF16), pltpu.VMEM((D, D), BF16)],
        compiler_params=_params(("arbitrary",)),
        name="qkv",
    )(y, ada, norm_mix.reshape(DEPTH, 1, D), wqkv,
      jnp.tile(q_norm[j], N_HEADS).reshape(1, D), jnp.tile(k_norm[j], N_KV).reshape(1, KV_WIDTH),
      cos, sin)


def _stack_heads(q, kv):
    return jnp.concatenate(
        [q[:, (kv * GQA + g) * HEAD_DIM:(kv * GQA + g + 1) * HEAD_DIM] for g in range(GQA)], axis=0)


def _sink_column(sink_ref, kv, rows):
    return jnp.concatenate(
        [jnp.broadcast_to(sink_ref[0:1, kv * GQA + g:kv * GQA + g + 1], (rows, 1)) for g in range(GQA)],
        axis=0)


def _qk(q, k):
    return lax.dot_general(q, k, (((1,), (1,)), ((), ())), preferred_element_type=F32) * (HEAD_DIM ** -0.5)


def _unstack_heads(o_groups, rows):
    return jnp.concatenate(
        [o[g * rows:(g + 1) * rows, :] for o in o_groups for g in range(GQA)], axis=1)


def _attn_ctx_kernel(q_ref, k_ref, v_ref, sink_ref, wo_ref, y_ref, ada_ref, o_ref, wob_ref):
    @pl.when(pl.program_id(0) == 0)
    def _():
        wob_ref[...] = wo_ref[...].astype(BF16)

    q = q_ref[...].astype(BF16)
    k = k_ref[...].astype(BF16)
    v = v_ref[...].astype(BF16)
    outs = []
    for kv in range(N_KV):
        sl = slice(kv * HEAD_DIM, (kv + 1) * HEAD_DIM)
        s = _qk(_stack_heads(q, kv), k[:, sl])
        sink = _sink_column(sink_ref, kv, CTX_LEN)
        m = jnp.maximum(jnp.max(s, axis=1, keepdims=True), sink)
        p = jnp.exp(s - m)
        den = jnp.sum(p, axis=1, keepdims=True) + jnp.exp(sink - m)
        outs.append(jnp.dot(p.astype(BF16), v[:, sl], preferred_element_type=F32) / den)
    o = _unstack_heads(outs, CTX_LEN)
    m_out = jnp.dot(o.astype(BF16), wob_ref[...], preferred_element_type=F32)
    o_ref[...] = y_ref[...] + ada_ref[2:3, :] * m_out


def _attn_ctx_call(q, k, v, sink, wo, y, ada, layer, j):
    return pl.pallas_call(
        _attn_ctx_kernel,
        out_shape=jax.ShapeDtypeStruct((T, D), F32),
        grid=(N_CTX_SEQ,),
        in_specs=[
            _tile_spec(), _tile_spec(KV_WIDTH), _tile_spec(KV_WIDTH),
            _layer_spec(j, (1, N_HEADS)),
            _layer_spec(j, (D, D)),
            _tile_spec(),
            _ada_spec(layer),
        ],
        out_specs=_tile_spec(),
        scratch_shapes=[pltpu.VMEM((D, D), BF16)],
        input_output_aliases={5: 0},
        compiler_params=_params(("arbitrary",)),
        name="attn_ctx",
    )(q, k, v, sink.reshape(-1, 1, N_HEADS), wo, y, ada)


LAT_WIN = 3 * WINDOW
N_QBLOCKS = LAT_LEN // Q_BLOCK


def _attn_lat_kernel(q_ref, k_ref, v_ref, kc_ref, vc_ref, sink_ref, wo_ref, y_ref, ada_ref,
                     o_ref, wob_ref):
    b = pl.program_id(0)
    n = pl.program_id(1)

    @pl.when((b == 0) & (n == 0))
    def _():
        wob_ref[...] = wo_ref[...].astype(BF16)

    start = pl.multiple_of(jnp.clip((n - 1) * WINDOW, 0, LAT_LEN - LAT_WIN), WINDOW)
    q = q_ref[...].astype(BF16)
    kw = k_ref[pl.ds(start, LAT_WIN), :].astype(BF16)
    vw = v_ref[pl.ds(start, LAT_WIN), :].astype(BF16)
    kc = kc_ref[...].astype(BF16)
    vc = vc_ref[...].astype(BF16)
    qpos = n * Q_BLOCK + lax.broadcasted_iota(I32, (GQA * Q_BLOCK, LAT_WIN), 0) % Q_BLOCK
    kpos = start + lax.broadcasted_iota(I32, (GQA * Q_BLOCK, LAT_WIN), 1)
    in_band = jnp.abs(kpos - qpos) <= WINDOW
    outs = []
    for kv in range(N_KV):
        sl = slice(kv * HEAD_DIM, (kv + 1) * HEAD_DIM)
        qg = _stack_heads(q, kv)
        s_loc = jnp.where(in_band, _qk(qg, kw[:, sl]), -jnp.inf)
        s_ctx = _qk(qg, kc[:, sl])
        sink = _sink_column(sink_ref, kv, Q_BLOCK)
        m = jnp.maximum(jnp.maximum(jnp.max(s_loc, axis=1, keepdims=True),
                                    jnp.max(s_ctx, axis=1, keepdims=True)), sink)
        p_loc = jnp.exp(s_loc - m)
        p_ctx = jnp.exp(s_ctx - m)
        den = (jnp.sum(p_loc, axis=1, keepdims=True) + jnp.sum(p_ctx, axis=1, keepdims=True)
               + jnp.exp(sink - m))
        o = (jnp.dot(p_loc.astype(BF16), vw[:, sl], preferred_element_type=F32)
             + jnp.dot(p_ctx.astype(BF16), vc[:, sl], preferred_element_type=F32))
        outs.append(o / den)
    o = _unstack_heads(outs, Q_BLOCK)
    m_out = jnp.dot(o.astype(BF16), wob_ref[...], preferred_element_type=F32)
    o_ref[...] = y_ref[...] + ada_ref[2:3, :] * m_out


def _attn_lat_call(q, k, v, cache_k, cache_v, sink, wo, y, ada, layer, j):
    q_row0 = TP // Q_BLOCK
    seq0 = TP // LAT_LEN
    qspec = pl.BlockSpec((Q_BLOCK, D), lambda b, n: (q_row0 + b * N_QBLOCKS + n, 0))
    kvspec = pl.BlockSpec((LAT_LEN, KV_WIDTH), lambda b, n: (seq0 + b, 0))
    cspec = pl.BlockSpec((None, None, CTX_LEN, KV_WIDTH), lambda b, n: (b, j, 0, 0))
    ck = cache_k.reshape(N_LAT_SEQ, -1, CTX_LEN, KV_WIDTH)
    cv = cache_v.reshape(N_LAT_SEQ, -1, CTX_LEN, KV_WIDTH)
    return pl.pallas_call(
        _attn_lat_kernel,
        out_shape=jax.ShapeDtypeStruct((T, D), F32),
        grid=(N_LAT_SEQ, N_QBLOCKS),
        in_specs=[
            qspec, kvspec, kvspec, cspec, cspec,
            _layer_spec(j, (1, N_HEADS)),
            _layer_spec(j, (D, D)),
            qspec,
            pl.BlockSpec((None, None, N_ADA, D), lambda b, n: (layer, 1 + b, 0, 0)),
        ],
        out_specs=qspec,
        scratch_shapes=[pltpu.VMEM((D, D), BF16)],
        input_output_aliases={7: 0},
        compiler_params=_params(("arbitrary", "arbitrary")),
        name="attn_lat",
    )(q, k, v, ck, cv, sink.reshape(-1, 1, N_HEADS), wo, y, ada)


def _router_kernel(y_ref, ada_ref, g_ref, rw_ref, rb_ref, sg_ref, su_ref, sd_ref,
                   h_ref, base_ref, key_ref, w_ref, cnt_ref,
                   carry_ref, tri_ref, sgb_ref, sub_ref, sdb_ref):
    i = pl.program_id(0)

    @pl.when(i == 0)
    def _():
        sgb_ref[...] = sg_ref[...].astype(BF16)
        sub_ref[...] = su_ref[...].astype(BF16)
        sdb_ref[...] = sd_ref[...].astype(BF16)
        r = lax.broadcasted_iota(I32, (ROW_TILE, ROW_TILE), 0)
        c = lax.broadcasted_iota(I32, (ROW_TILE, ROW_TILE), 1)
        tri_ref[...] = jnp.where(r < c, 1.0, 0.0).astype(BF16)

    @pl.when(i % TILES_PER_GROUP == 0)
    def _():
        carry_ref[...] = jnp.zeros_like(carry_ref)

    y = y_ref[...]
    h = _modulate(y, g_ref[...], ada_ref[3:4, :], ada_ref[4:5, :])
    for c in range(LANE_CHUNKS):
        h_ref[pl.ds(c, ROW_TILE, stride=LANE_CHUNKS), :] = h[:, c * LANES:(c + 1) * LANES]
    hb = h.astype(BF16)

    d = functools.partial(jnp.dot, preferred_element_type=F32)
    act = (_silu(d(hb, sgb_ref[...])) * d(hb, sub_ref[...])).astype(BF16)
    base_ref[...] = y + ada_ref[5:6, :] * d(act, sdb_ref[...])

    logits = _dot3(h, rw_ref[...])
    logits_t = jnp.concatenate([logits, jnp.zeros_like(logits)], axis=1).T[:N_EXPERTS]
    scores = jax.nn.sigmoid(logits_t)
    shape3 = (N_EGROUPS, EGROUP, ROW_TILE)
    scores3 = scores.reshape(shape3)
    choice3 = (scores + rb_ref[...]).reshape(shape3)
    sub = lax.broadcasted_iota(I32, shape3, 1)
    eidx = lax.broadcasted_iota(I32, shape3, 0) * EGROUP + sub
    gidx = lax.broadcasted_iota(I32, (N_EGROUPS, 1, ROW_TILE), 0)
    neg = -jnp.inf

    m1 = jnp.max(choice3, axis=1, keepdims=True)
    i1 = jnp.min(jnp.where(choice3 == m1, sub, EGROUP), axis=1, keepdims=True)
    m2 = jnp.max(jnp.where(sub == i1, neg, choice3), axis=1, keepdims=True)
    gscore = m1 + m2
    allowed = jnp.zeros((N_EGROUPS, 1, ROW_TILE), jnp.bool_)
    for _ in range(TOPK_GROUPS):
        gm = jnp.max(gscore, axis=0, keepdims=True)
        gi = jnp.min(jnp.where(gscore == gm, gidx, N_EGROUPS), axis=0, keepdims=True)
        hit = gidx == gi
        allowed = allowed | hit
        gscore = jnp.where(hit, neg, gscore)

    def reduce_experts(fn, x):
        return fn(fn(x, axis=0, keepdims=True), axis=1, keepdims=True)

    cm = jnp.where(allowed, choice3, neg)
    picked = jnp.zeros(shape3, F32)
    e_rows, s_rows = [], []
    for _ in range(TOP_K):
        m = reduce_experts(jnp.max, cm)
        ik = reduce_experts(jnp.min, jnp.where(cm == m, eidx, N_EXPERTS))
        hit = eidx == ik
        e_rows.append(ik)
        s_rows.append(reduce_experts(jnp.sum, jnp.where(hit, scores3, 0.0)))
        cm = jnp.where(hit, neg, cm)
        picked = jnp.where(hit, 1.0, picked)

    picked2 = picked.reshape(N_EXPERTS, ROW_TILE)
    cum3 = (d(picked2.astype(BF16), tri_ref[...]) + carry_ref[...]).reshape(shape3)
    carry_ref[...] = carry_ref[...] + jnp.sum(picked2, axis=1, keepdims=True)
    cnt_ref[...] = jnp.broadcast_to(carry_ref[...], (N_EXPERTS, LANES))

    denom = functools.reduce(lambda a, b: a + b, s_rows)
    key_rows, w_rows = [], []
    for kk in range(TOP_K):
        rank = reduce_experts(jnp.sum, jnp.where(eidx == e_rows[kk], cum3, 0.0)).astype(I32)
        key_rows.append(((e_rows[kk] << RANK_BITS) | rank).reshape(1, ROW_TILE))
        w_rows.append((s_rows[kk] / denom * ROUTED_SCALE).reshape(1, ROW_TILE))
    key_ref[...] = jnp.concatenate(key_rows, axis=0)
    w_ref[...] = jnp.concatenate(w_rows, axis=0)


def _router_call(y, ada, norm_ffn, router_w, router_bias, sg, su, sd, layer):
    return pl.pallas_call(
        _router_kernel,
        out_shape=(
            jax.ShapeDtypeStruct((T * LANE_CHUNKS, LANES), F32),
            jax.ShapeDtypeStruct((T, D), F32),
            jax.ShapeDtypeStruct((TOP_K, T), I32),
            jax.ShapeDtypeStruct((TOP_K, T), F32),
            jax.ShapeDtypeStruct((N_TILES, N_EXPERTS, LANES), F32),
        ),
        grid=(N_TILES,),
        in_specs=[
            _tile_spec(),
            _ada_spec(layer),
            _layer_spec(layer, (1, D)),
            _layer_spec(layer, (D, N_EXPERTS)),
            _layer_spec(layer, (N_EXPERTS, 1)),
            _layer_spec(layer, (D, D_SHARED)),
            _layer_spec(layer, (D, D_SHARED)),
            _layer_spec(layer, (D_SHARED, D)),
        ],
        out_specs=(
            pl.BlockSpec((ROW_TILE * LANE_CHUNKS, LANES), lambda i: (i, 0)),
            _tile_spec(),
            pl.BlockSpec((TOP_K, ROW_TILE), lambda i: (0, i)),
            pl.BlockSpec((TOP_K, ROW_TILE), lambda i: (0, i)),
            pl.BlockSpec((None, N_EXPERTS, LANES), lambda i: (i, 0, 0)),
        ),
        scratch_shapes=[
            pltpu.VMEM((N_EXPERTS, 1), F32),
            pltpu.VMEM((ROW_TILE, ROW_TILE), BF16),
            pltpu.VMEM((D, D_SHARED), BF16),
            pltpu.VMEM((D, D_SHARED), BF16),
            pltpu.VMEM((D_SHARED, D), BF16),
        ],
        compiler_params=_params(("arbitrary",)),
        name="router",
    )(y, ada, norm_ffn.reshape(DEPTH, 1, D), router_w, router_bias.reshape(DEPTH, N_EXPERTS, 1),
      sg, su, sd)


SLOT_ROWS = MOE_SLOTS // LANES
ASSIGN_ROWS = N_ASSIGN // LANES
LANE_BITS = 7


def _pos_kernel(start_ref, key_ref, pos_ref):
    g = pl.program_id(0)
    key = key_ref[...]
    e = key >> RANK_BITS
    pos = key & ((1 << RANK_BITS) - 1)
    for ee in range(N_EXPERTS):
        pos = pos + jnp.where(e == ee, start_ref[g * N_EXPERTS + ee], 0)
    pos_ref[...] = pos


def _pos_call(starts, keys):
    spec = pl.BlockSpec((TOP_K, MOE_GROUP), lambda g, s: (0, g))
    return pl.pallas_call(
        _pos_kernel,
        out_shape=jax.ShapeDtypeStruct((TOP_K, T), I32),
        grid_spec=pltpu.PrefetchScalarGridSpec(
            num_scalar_prefetch=1, grid=(N_GROUPS,), in_specs=[spec], out_specs=spec),
        compiler_params=_params(("arbitrary",)),
        name="moe_pos",
    )(starts, keys)


SLOT_ROW_BITS = 16
DUMMY_SLOT = MOE_GROUP * SUBLANES
SLOT_UNROLL = 4


def _slot_kernel(pos_ref, fill_lo_ref, fill_hi_ref, slot_ref):
    per_token = SUBLANES + (1 << SLOT_ROW_BITS)

    def place(t, carry):
        word = t * per_token
        for k in range(TOP_K):
            slot_ref[pos_ref[k, t]] = word + ((k * MOE_GROUP) << SLOT_ROW_BITS)
        return carry

    lax.fori_loop(0, MOE_GROUP, place, 0, unroll=SLOT_UNROLL)

    def fill_range(e, carry):
        def fill(p, c):
            slot_ref[p] = DUMMY_SLOT
            return c

        lax.fori_loop(fill_lo_ref[e], fill_hi_ref[e], fill, 0)
        return carry

    lax.fori_loop(0, N_EXPERTS, fill_range, 0)


def _slot_call(pos, fill_lo, fill_hi):
    smem = pl.BlockSpec(memory_space=pltpu.SMEM)
    return pl.pallas_call(
        _slot_kernel,
        out_shape=jax.ShapeDtypeStruct((MOE_SLOTS,), I32),
        in_specs=[smem, smem, smem],
        out_specs=smem,
        name="moe_slots",
    )(pos, fill_lo, fill_hi)


TILE_ROWS = MOE_GROUP * SUBLANES
ACC_TILE_ROWS = ACC_ROWS * SUBLANES
RMW_ROWS = 16


def _expert_kernel(blk_exp_ref, n_used_ref,
                   slot_hbm, w_hbm, x_hbm, wg_ref, wu_ref, wd_ref,
                   out_hbm,
                   x_vmem, acc_vmem, tile_ref, wgb_ref, wub_ref, wdb_ref, slot_smem, w_smem, sems):
    g = pl.program_id(0)
    j = pl.program_id(1)
    idx = g * MOE_NB + j

    def group_copies():
        return (pltpu.make_async_copy(x_hbm.at[g], x_vmem.at[pl.ds(0, TILE_ROWS)], sems.at[0]),
                pltpu.make_async_copy(slot_hbm.at[g], slot_smem, sems.at[1]),
                pltpu.make_async_copy(w_hbm.at[g], w_smem, sems.at[2]))

    @pl.when(j == 0)
    def _():
        for c in group_copies():
            c.start()
        acc_vmem[...] = jnp.zeros_like(acc_vmem)
        x_vmem[pl.ds(TILE_ROWS, ACC_TILE_ROWS - TILE_ROWS), :] = jnp.zeros(
            (ACC_TILE_ROWS - TILE_ROWS, LANES), F32)
        for c in group_copies():
            c.wait()

    e_cur = blk_exp_ref[idx]
    e_prev = blk_exp_ref[jnp.maximum(idx - 1, 0)]

    @pl.when((j == 0) | (e_cur != e_prev))
    def _():
        wgb_ref[...] = wg_ref[...].astype(BF16)
        wub_ref[...] = wu_ref[...].astype(BF16)
        wdb_ref[...] = wd_ref[...].astype(BF16)

    @pl.when(j < n_used_ref[g])
    def _():
        rows_per_block = MOE_BM // LANES

        def slot_word(r):
            return slot_smem[j * rows_per_block + r // LANES, r % LANES]

        def tile_rows(word):
            return pl.ds(pl.multiple_of(word & ((1 << SLOT_ROW_BITS) - 1), SUBLANES), SUBLANES)

        for r in range(MOE_BM):
            tile_ref[pl.ds(r * SUBLANES, SUBLANES), :] = x_vmem[tile_rows(slot_word(r)), :]
        xs = jnp.concatenate(
            [tile_ref[pl.ds(c, MOE_BM, stride=LANE_CHUNKS), :] for c in range(LANE_CHUNKS)],
            axis=1).astype(BF16)
        d = functools.partial(jnp.dot, preferred_element_type=F32)
        act = (_silu(d(xs, wgb_ref[...])) * d(xs, wub_ref[...])).astype(BF16)
        yb = d(act, wdb_ref[...])
        for c in range(LANE_CHUNKS):
            tile_ref[pl.ds(c, MOE_BM, stride=LANE_CHUNKS), :] = yb[:, c * LANES:(c + 1) * LANES]
        for r0 in range(0, MOE_BM, RMW_ROWS):
            words = [slot_word(r) for r in range(r0, r0 + RMW_ROWS)]
            new = [acc_vmem[tile_rows(word), :]
                   + w_smem[word >> SLOT_ROW_BITS] * tile_ref[pl.ds(r * SUBLANES, SUBLANES), :]
                   for r, word in zip(range(r0, r0 + RMW_ROWS), words)]
            for word, tile in zip(words, new):
                acc_vmem[tile_rows(word), :] = tile

    @pl.when(j == MOE_NB - 1)
    def _():
        out_copy = pltpu.make_async_copy(acc_vmem, out_hbm.at[g], sems.at[3])
        out_copy.start()
        out_copy.wait()


def _expert_call(blk_exp, n_used, slots, ws, x_tiles, wg, wu, wd, layer):
    def wspec(shape):
        return pl.BlockSpec((None, None) + shape,
                            lambda g, j, be, nu: (layer, be[g * MOE_NB + j], 0, 0))

    any_spec = pl.BlockSpec(memory_space=pl.ANY)
    return pl.pallas_call(
        _expert_kernel,
        out_shape=jax.ShapeDtypeStruct((N_GROUPS, ACC_TILE_ROWS, LANES), F32),
        grid_spec=pltpu.PrefetchScalarGridSpec(
            num_scalar_prefetch=2,
            grid=(N_GROUPS, MOE_NB),
            in_specs=[any_spec, any_spec, any_spec,
                      wspec((D, D_EXPERT)), wspec((D, D_EXPERT)), wspec((D_EXPERT, D))],
            out_specs=any_spec,
            scratch_shapes=[
                pltpu.VMEM((ACC_TILE_ROWS, LANES), F32),
                pltpu.VMEM((ACC_TILE_ROWS, LANES), F32),
                pltpu.VMEM((MOE_BM * SUBLANES, LANES), F32),
                pltpu.VMEM((D, D_EXPERT), BF16),
                pltpu.VMEM((D, D_EXPERT), BF16),
                pltpu.VMEM((D_EXPERT, D), BF16),
                pltpu.SMEM((SLOT_ROWS, LANES), I32),
                pltpu.SMEM((N_ASSIGN,), F32),
                pltpu.SemaphoreType.DMA((4,)),
            ]),
        compiler_params=_params(("arbitrary", "arbitrary")),
        name="moe_experts",
    )(blk_exp, n_used, slots, ws, x_tiles, wg, wu, wd)


def _finish_kernel(base_ref, routed_ref, ada_ref, o_ref):
    routed = jnp.concatenate(
        [routed_ref[pl.ds(c, ROW_TILE, stride=LANE_CHUNKS), :] for c in range(LANE_CHUNKS)], axis=1)
    o_ref[...] = base_ref[...] + ada_ref[5:6, :] * routed


def _finish_call(base, routed, ada, layer):
    rows = ROW_TILE * LANE_CHUNKS
    return pl.pallas_call(
        _finish_kernel,
        out_shape=jax.ShapeDtypeStruct((T, D), F32),
        grid=(N_TILES,),
        in_specs=[
            _tile_spec(),
            pl.BlockSpec((None, rows, LANES), lambda i: (i // TILES_PER_GROUP, i % TILES_PER_GROUP, 0)),
            _ada_spec(layer),
        ],
        out_specs=_tile_spec(),
        compiler_params=_params(("arbitrary",)),
        name="moe_finish",
    )(base, routed, ada)


def _moe_layer(y, ada, norm_ffn, router_w, router_bias, wg, wu, wd, sg, su, sd, layer):
    h_tiles, base, keys, ws, cnts = _router_call(
        y, ada, norm_ffn, router_w, router_bias, sg, su, sd, layer)

    counts = cnts[TILES_PER_GROUP - 1::TILES_PER_GROUP, :, 0].astype(I32)
    padded = (counts + MOE_BM - 1) // MOE_BM * MOE_BM
    pad_end = jnp.cumsum(padded, axis=1)
    starts = pad_end - padded
    n_used = (pad_end[:, -1] // MOE_BM).astype(I32)
    blocks = jnp.arange(MOE_NB, dtype=I32) * MOE_BM
    blk_exp = jnp.minimum(
        jnp.sum(blocks[None, :, None] >= pad_end[:, None, :], axis=2), N_EXPERTS - 1).astype(I32)
    fill_lo = starts + counts
    fill_hi = pad_end.at[:, -1].set(MOE_SLOTS)

    pos = _pos_call(starts.reshape(-1), keys)
    slots = jnp.stack([
        _slot_call(pos[:, g * MOE_GROUP:(g + 1) * MOE_GROUP], fill_lo[g], fill_hi[g])
        for g in range(N_GROUPS)]).reshape(N_GROUPS, SLOT_ROWS, LANES)
    ws_g = ws.reshape(TOP_K, N_GROUPS, MOE_GROUP).transpose(1, 0, 2).reshape(N_GROUPS, N_ASSIGN)
    x_tiles = h_tiles.reshape(N_GROUPS, TILE_ROWS, LANES)
    routed = _expert_call(blk_exp.reshape(-1), n_used, slots, ws_g, x_tiles, wg, wu, wd, layer)
    return _finish_call(base, routed, ada, layer)


def kernel(x_prompt, x_sample, cache_k, cache_v, c, c_ctx, ada_w, ada_b, norm_mix, norm_ffn,
           conv_w1, conv_b1, conv_dw, conv_dw_b, conv_norm, conv_w2, conv_b2,
           attn_wqkv, attn_q_norm, attn_k_norm, attn_sink, attn_wo,
           router_w, router_bias, exp_w_gate, exp_w_up, exp_w_down,
           sh_w_gate, sh_w_up, sh_w_down):
    y = jnp.concatenate([x_prompt.reshape(TP, D), x_sample.reshape(TS, D)], axis=0)
    cond = jnp.concatenate(
        [c_ctx[None, :], c, jnp.zeros((COND_ROWS - N_COND, D), F32)], axis=0)
    ada = _ada_call(cond, ada_w, ada_b).reshape(DEPTH, COND_ROWS, N_ADA, D)

    new_k = new_v = None
    for layer in range(DEPTH):
        j = layer // 2
        if layer % 2 == 0:
            u = _conv_in_call(y, ada, norm_mix, conv_w1, conv_b1, layer, j)
            y = _conv_out_call(u, y, ada, conv_dw, conv_dw_b, conv_norm, conv_w2, conv_b2, layer, j)
        else:
            q, k, v = _qkv_call(y, ada, norm_mix, attn_wqkv, attn_q_norm, attn_k_norm, layer, j)
            new_k = k[:TP].reshape(N_CTX_SEQ, 1, CTX_LEN, N_KV, HEAD_DIM)
            new_v = v[:TP].reshape(N_CTX_SEQ, 1, CTX_LEN, N_KV, HEAD_DIM)
            y = _attn_ctx_call(q, k, v, attn_sink, attn_wo, y, ada, layer, j)
            y = _attn_lat_call(q, k, v, cache_k, cache_v, attn_sink, attn_wo, y, ada, layer, j)
        y = _moe_layer(y, ada, norm_ffn, router_w, router_bias, exp_w_gate, exp_w_up, exp_w_down,
                       sh_w_gate, sh_w_up, sh_w_down, layer)

    y_p = y[:TP].reshape(N_CTX_SEQ, CTX_LEN, D)
    y_s = y[TP:].reshape(N_LAT_SEQ, LAT_LEN, D)
    return (y_p, y_s, new_k, new_v)
```

```python
import functools

import jax
import jax.numpy as jnp
import numpy as np
from jax import lax
from jax.experimental import pallas as pl
from jax.experimental.pallas import tpu as pltpu

F32 = jnp.float32
BF16 = jnp.bfloat16
I32 = jnp.int32

D = 1024
N_CTX_SEQ = 32
CTX_LEN = 256
N_LAT_SEQ = 4
LAT_LEN = 1024
TP = N_CTX_SEQ * CTX_LEN
TS = N_LAT_SEQ * LAT_LEN
T = TP + TS
DEPTH = 2
N_ADA = 6
N_COND = 1 + N_LAT_SEQ
COND_ROWS = 8
CONV_WIDTH = 31
CONV_PAD = CONV_WIDTH // 2
N_HEADS = 16
N_KV = 4
HEAD_DIM = 64
GQA = N_HEADS // N_KV
KV_WIDTH = N_KV * HEAD_DIM
QKV_WIDTH = D + 2 * KV_WIDTH
WINDOW = 128
GRID_W = 64
ROPE_PAIRS = HEAD_DIM // 4
ROPE_THETA = 10000.0
N_EXPERTS = 64
N_EGROUPS = 8
EGROUP = N_EXPERTS // N_EGROUPS
TOPK_GROUPS = 4
TOP_K = 8
D_EXPERT = 256
D_SHARED = 256
ROUTED_SCALE = 2.5
NORM_EPS = 1e-6

LANES = 128
SUBLANES = 8
VMEM_LIMIT = 56 * 1024 * 1024

ROW_TILE = 256
N_TILES = T // ROW_TILE
N_CTX_TILES = TP // ROW_TILE
LAT_TILES_PER_SEQ = LAT_LEN // ROW_TILE
HALO = 16
LANE_CHUNKS = D // LANES

MOE_GROUP = 4096
N_GROUPS = T // MOE_GROUP
TILES_PER_GROUP = MOE_GROUP // ROW_TILE
MOE_BM = 256
MOE_NB = MOE_GROUP * TOP_K // MOE_BM + N_EXPERTS
N_ASSIGN = MOE_GROUP * TOP_K
ACC_ROWS = MOE_GROUP + SUBLANES
RANK_BITS = 12
Q_BLOCK = 128


def _cond_of_tile(i):
    return jnp.where(i < N_CTX_TILES, 0, 1 + (i - N_CTX_TILES) // LAT_TILES_PER_SEQ)


def _params(sem, vmem=VMEM_LIMIT):
    return pltpu.CompilerParams(dimension_semantics=sem, vmem_limit_bytes=vmem)


def _bdot(a, b):
    return jnp.dot(a.astype(BF16), b.astype(BF16), preferred_element_type=F32)


def _split(a):
    hi = a.astype(BF16)
    lo = (a - hi.astype(F32)).astype(BF16)
    return hi, lo


def _dot3(a, b):
    a_hi, a_lo = _split(a)
    b_hi, b_lo = _split(b)
    d = functools.partial(jnp.dot, preferred_element_type=F32)
    return d(a_hi, b_hi) + d(a_lo, b_hi) + d(a_hi, b_lo)


def _dot2(a, b_bf16):
    a_hi, a_lo = _split(a)
    d = functools.partial(jnp.dot, preferred_element_type=F32)
    return d(a_hi, b_bf16) + d(a_lo, b_bf16)


def _rms(x, g):
    return x * lax.rsqrt(jnp.mean(x * x, axis=-1, keepdims=True) + NORM_EPS) * g


def _modulate(x, g, shift, scale):
    return _rms(x, g) * (1.0 + scale) + shift


def _silu(x):
    return x * jax.nn.sigmoid(x)


def _ada_spec(layer):
    return pl.BlockSpec((None, None, N_ADA, D), lambda i: (layer, _cond_of_tile(i), 0, 0))


def _tile_spec(width=D):
    return pl.BlockSpec((ROW_TILE, width), lambda i: (i, 0))


STREAM_SPECS = [
    pl.BlockSpec((ROW_TILE, D), lambda i: (jnp.minimum(i, N_CTX_TILES - 1), 0)),
    pl.BlockSpec((ROW_TILE, D), lambda i: (jnp.maximum(i - N_CTX_TILES, 0), 0)),
]
STREAM_SHAPES = (jax.ShapeDtypeStruct((TP, D), F32), jax.ShapeDtypeStruct((TS, D), F32))


def _read_stream(ctx_ref, lat_ref):
    return jnp.where(pl.program_id(0) < N_CTX_TILES, ctx_ref[...], lat_ref[...])


def _write_stream(ctx_ref, lat_ref, value):
    i = pl.program_id(0)

    @pl.when(i < N_CTX_TILES)
    def _():
        ctx_ref[...] = value

    @pl.when(i >= N_CTX_TILES)
    def _():
        lat_ref[...] = value


def _const_spec(shape):
    nd = len(shape)
    return pl.BlockSpec(shape, lambda *_: (0,) * nd)


def _layer_spec(layer, shape):
    nd = len(shape)
    return pl.BlockSpec((None,) + tuple(shape), lambda *_: (layer,) + (0,) * nd)


ADA_NB = 512


def _ada_kernel(c_ref, w_ref, b_ref, o_ref):
    o_ref[...] = _dot3(_silu(c_ref[...]), w_ref[...]) + b_ref[...]


def _ada_call(cond, ada_w, ada_b):
    return pl.pallas_call(
        _ada_kernel,
        out_shape=jax.ShapeDtypeStruct((DEPTH, COND_ROWS, N_ADA * D), F32),
        grid=(DEPTH, N_ADA * D // ADA_NB),
        in_specs=[
            pl.BlockSpec((COND_ROWS, D), lambda l, n: (0, 0)),
            pl.BlockSpec((None, D, ADA_NB), lambda l, n: (l, 0, n)),
            pl.BlockSpec((None, 1, ADA_NB), lambda l, n: (l, 0, n)),
        ],
        out_specs=pl.BlockSpec((None, COND_ROWS, ADA_NB), lambda l, n: (l, 0, n)),
        compiler_params=_params(("arbitrary", "arbitrary")),
        name="ada_params",
    )(cond, ada_w, ada_b.reshape(DEPTH, 1, N_ADA * D))


def _conv_in_kernel(xc_ref, xl_ref, ada_ref, g_ref, w1_ref, b1_ref, u_ref, w1b_ref):
    @pl.when(pl.program_id(0) == 0)
    def _():
        w1b_ref[...] = w1_ref[...].astype(BF16)

    h = _modulate(_read_stream(xc_ref, xl_ref), g_ref[...], ada_ref[0:1, :], ada_ref[1:2, :])
    u = jnp.dot(h.astype(BF16), w1b_ref[...], preferred_element_type=F32) + b1_ref[...]
    u_ref[...] = u[:, :D] * jax.nn.sigmoid(u[:, D:])


def _conv_in_call(y, ada, norm_mix, conv_w1, conv_b1, layer, j):
    return pl.pallas_call(
        _conv_in_kernel,
        out_shape=jax.ShapeDtypeStruct((T, D), F32),
        grid=(N_TILES,),
        in_specs=STREAM_SPECS + [
            _ada_spec(layer),
            _layer_spec(layer, (1, D)),
            _layer_spec(j, (D, 2 * D)),
            _layer_spec(j, (1, 2 * D)),
        ],
        out_specs=_tile_spec(),
        scratch_shapes=[pltpu.VMEM((D, 2 * D), BF16)],
        compiler_params=_params(("arbitrary",)),
        name="conv_in",
    )(*y, ada, norm_mix.reshape(DEPTH, 1, D), conv_w1, conv_b1.reshape(-1, 1, 2 * D))


CONV_ROWS = 64
BUF_ROWS = ROW_TILE + 2 * HALO


def _conv_out_kernel(u_ref, up_ref, un_ref, yc_ref, yl_ref, ada_ref, dw_ref, dwb_ref, gn_ref,
                     w2_ref, b2_ref, oc_ref, ol_ref, buf_ref, z_ref, w2b_ref):
    i = pl.program_id(0)

    @pl.when(i == 0)
    def _():
        w2b_ref[...] = w2_ref[...].astype(BF16)

    lat = i >= N_CTX_TILES
    pos = jnp.where(lat, i - N_CTX_TILES, 0) % LAT_TILES_PER_SEQ
    has_prev = lat & (pos != 0)
    has_next = lat & (pos != LAT_TILES_PER_SEQ - 1)
    for c in range(LANE_CHUNKS):
        cs = slice(c * LANES, (c + 1) * LANES)
        buf_ref[c, 0:HALO, :] = jnp.where(has_prev, up_ref[:, cs], 0.0)
        buf_ref[c, HALO:HALO + ROW_TILE, :] = u_ref[:, cs]
        buf_ref[c, HALO + ROW_TILE:BUF_ROWS, :] = jnp.where(has_next, un_ref[:, cs], 0.0)

    off = HALO - CONV_PAD
    for c in range(LANE_CHUNKS):
        cs = slice(c * LANES, (c + 1) * LANES)
        for r0 in range(0, ROW_TILE, CONV_ROWS):
            acc = jnp.broadcast_to(dwb_ref[:, cs], (CONV_ROWS, LANES))
            for k in range(CONV_WIDTH):
                win = buf_ref[c, r0 + k + off:r0 + k + off + CONV_ROWS, :]
                acc = acc + dw_ref[k:k + 1, cs] * win
            z_ref[r0:r0 + CONV_ROWS, cs] = acc

    z = _silu(_rms(z_ref[...], gn_ref[...]))
    m = jnp.dot(z.astype(BF16), w2b_ref[...], preferred_element_type=F32) + b2_ref[...]
    _write_stream(oc_ref, ol_ref, _read_stream(yc_ref, yl_ref) + ada_ref[2:3, :] * m)


def _conv_out_call(u, y, ada, conv_dw, conv_dw_b, conv_norm, conv_w2, conv_b2, layer, j):
    halos_per_tile = ROW_TILE // HALO
    last_halo = T // HALO - 1
    return pl.pallas_call(
        _conv_out_kernel,
        out_shape=STREAM_SHAPES,
        grid=(N_TILES,),
        in_specs=[
            _tile_spec(),
            pl.BlockSpec((HALO, D), lambda i: (jnp.maximum(i * halos_per_tile - 1, 0), 0)),
            pl.BlockSpec((HALO, D), lambda i: (jnp.minimum((i + 1) * halos_per_tile, last_halo), 0)),
        ] + STREAM_SPECS + [
            _ada_spec(layer),
            _layer_spec(j, (CONV_WIDTH + 1, D)),
            _layer_spec(j, (1, D)),
            _layer_spec(j, (1, D)),
            _layer_spec(j, (D, D)),
            _layer_spec(j, (1, D)),
        ],
        out_specs=STREAM_SPECS,
        scratch_shapes=[pltpu.VMEM((LANE_CHUNKS, BUF_ROWS, LANES), F32), pltpu.VMEM((ROW_TILE, D), F32),
                        pltpu.VMEM((D, D), BF16)],
        compiler_params=_params(("arbitrary",)),
        name="conv_out",
    )(u, u, u, *y, ada, jnp.pad(conv_dw, ((0, 0), (0, 1), (0, 0))), conv_dw_b.reshape(-1, 1, D),
      conv_norm.reshape(-1, 1, D), conv_w2, conv_b2.reshape(-1, 1, D))


def _swap_pairs(x, width):
    lane = lax.broadcasted_iota(I32, x.shape, 1)
    first = (lane % (2 * ROPE_PAIRS)) < ROPE_PAIRS
    return jnp.where(first, pltpu.roll(x, width - ROPE_PAIRS, 1), pltpu.roll(x, ROPE_PAIRS, 1))


def _qkv_kernel(xc_ref, xl_ref, ada_ref, g_ref, w_ref, qn_ref, kn_ref, cos_ref, sin_ref,
                q_ref, k_ref, v_ref, wb_ref, hs_ref):
    i = pl.program_id(0)

    @pl.when(i == 0)
    def _():
        wb_ref[...] = w_ref[...].astype(BF16)
        r = lax.broadcasted_iota(I32, (D, D), 0) // HEAD_DIM
        c = lax.broadcasted_iota(I32, (D, D), 1) // HEAD_DIM
        hs_ref[...] = jnp.where(r == c, 1.0 / HEAD_DIM, 0.0).astype(BF16)

    h = _modulate(_read_stream(xc_ref, xl_ref), g_ref[...], ada_ref[0:1, :], ada_ref[1:2, :])
    qkv = jnp.dot(h.astype(BF16), wb_ref[...], preferred_element_type=F32)
    q = qkv[:, :D]
    k = qkv[:, D:D + KV_WIDTH]
    v_ref[...] = qkv[:, D + KV_WIDTH:]
    q_ms = _dot2(q * q, hs_ref[...])
    k_ms = _dot2(k * k, hs_ref[0:KV_WIDTH, 0:KV_WIDTH])
    qn = q * lax.rsqrt(q_ms + NORM_EPS) * qn_ref[...]
    kn = k * lax.rsqrt(k_ms + NORM_EPS) * kn_ref[...]

    @pl.when(i < N_CTX_TILES)
    def _():
        q_ref[...] = qn
        k_ref[...] = kn

    @pl.when(i >= N_CTX_TILES)
    def _():
        cos = cos_ref[...]
        sin = sin_ref[...]
        q_ref[...] = qn * cos + _swap_pairs(qn, D) * sin
        k_ref[...] = kn * cos[:, :KV_WIDTH] + _swap_pairs(kn, KV_WIDTH) * sin[:, :KV_WIDTH]


def _rope_tables():
    pos = np.arange(LAT_LEN)
    row = (pos // GRID_W).astype(np.float32)
    col = (pos % GRID_W).astype(np.float32)
    inv_freq = np.float32(ROPE_THETA) ** (-np.arange(ROPE_PAIRS, dtype=np.float32) / ROPE_PAIRS)
    d = np.arange(D) % HEAD_DIM
    freq = inv_freq[d % ROPE_PAIRS].astype(np.float32)
    p = np.where((d >= HEAD_DIM // 2)[None, :], col[:, None], row[:, None])
    ang = (p * freq[None, :]).astype(np.float32)
    first = (d % (2 * ROPE_PAIRS)) < ROPE_PAIRS
    cos = np.cos(ang).astype(np.float32)
    sin = np.sin(ang).astype(np.float32)
    return jnp.asarray(cos), jnp.asarray(np.where(first[None, :], -sin, sin))


def _qkv_call(y, ada, norm_mix, wqkv, q_norm, k_norm, layer, j):
    cos, sin = _rope_tables()
    table_spec = pl.BlockSpec(
        (ROW_TILE, D), lambda i: (jnp.maximum(i - N_CTX_TILES, 0) % LAT_TILES_PER_SEQ, 0))
    return pl.pallas_call(
        _qkv_kernel,
        out_shape=(jax.ShapeDtypeStruct((T, D), F32), jax.ShapeDtypeStruct((T, KV_WIDTH), F32),
                   jax.ShapeDtypeStruct((T, KV_WIDTH), F32)),
        grid=(N_TILES,),
        in_specs=STREAM_SPECS + [
            _ada_spec(layer),
            _layer_spec(layer, (1, D)),
            _layer_spec(j, (D, QKV_WIDTH)),
            _const_spec((1, D)),
            _const_spec((1, KV_WIDTH)),
            table_spec,
            table_spec,
        ],
        out_specs=(_tile_spec(), _tile_spec(KV_WIDTH), _tile_spec(KV_WIDTH)),
        scratch_shapes=[pltpu.VMEM((D, QKV_WIDTH), BF16), pltpu.VMEM((D, D), BF16)],
        compiler_params=_params(("arbitrary",)),
        name="qkv",
    )(*y, ada, norm_mix.reshape(DEPTH, 1, D), wqkv,
      jnp.tile(q_norm[j], N_HEADS).reshape(1, D), jnp.tile(k_norm[j], N_KV).reshape(1, KV_WIDTH),
      cos, sin)


HEADS_PER_TILE = LANES // HEAD_DIM
TILES_PER_KV = GQA // HEADS_PER_TILE


def _gqa_attention(q, segments, sink_ref, rows):
    d = functools.partial(jnp.dot, preferred_element_type=F32)
    half = lax.broadcasted_iota(I32, (rows, LANES), 1) < HEAD_DIM
    q_zero = jnp.zeros((rows, LANES), BF16)

    scores, sinks, kvs = [], [], []
    for kv in range(N_KV):
        sl = slice(kv * HEAD_DIM, (kv + 1) * HEAD_DIM)
        parts = []
        for t in range(TILES_PER_KV):
            tile = kv * TILES_PER_KV + t
            qt = q[:, tile * LANES:(tile + 1) * LANES]
            parts += [jnp.where(half, qt, q_zero), jnp.where(half, q_zero, qt)]
        qg = jnp.concatenate(parts, axis=0)
        seg_scores, seg_kv = [], []
        for k, v, bias in segments:
            kk = jnp.concatenate([k[:, sl], k[:, sl]], axis=1)
            s = lax.dot_general(qg, kk, (((1,), (1,)), ((), ())),
                                preferred_element_type=F32) * (HEAD_DIM ** -0.5)
            seg_scores.append(s if bias is None else s + bias)
            v_zero = jnp.zeros_like(v[:, sl])
            seg_kv.append((jnp.concatenate([v[:, sl], v_zero], axis=1),
                           jnp.concatenate([v_zero, v[:, sl]], axis=1)))
        scores.append(seg_scores)
        kvs.append(seg_kv)
        sinks.append(jnp.concatenate(
            [jnp.broadcast_to(sink_ref[0:1, kv * GQA + g:kv * GQA + g + 1], (rows, 1))
             for g in range(GQA)], axis=0))

    maxes = []
    for kv in range(N_KV):
        m = sinks[kv]
        for s in scores[kv]:
            m = jnp.maximum(m, jnp.max(s, axis=1, keepdims=True))
        maxes.append(m)
    probs = [[jnp.exp(s - maxes[kv]).astype(BF16) for s in scores[kv]] for kv in range(N_KV)]

    tiles = []
    for kv in range(N_KV):
        den = jnp.exp(sinks[kv] - maxes[kv])
        for p in probs[kv]:
            den = den + d(p, jnp.ones((p.shape[1], LANES), BF16))
        inv = 1.0 / den
        for t in range(TILES_PER_KV):
            tile = None
            for h in range(HEADS_PER_TILE):
                g = t * HEADS_PER_TILE + h
                rs = slice(g * rows, (g + 1) * rows)
                o = None
                for p, v_halves in zip(probs[kv], kvs[kv]):
                    part = d(p[rs, :], v_halves[h])
                    o = part if o is None else o + part
                o = o * inv[rs, :]
                tile = o if tile is None else tile + o
            tiles.append(tile)
    return jnp.concatenate(tiles, axis=1)


def _attn_ctx_kernel(q_ref, k_ref, v_ref, sink_ref, wo_ref, y_ref, ada_ref, o_ref, wob_ref):
    @pl.when(pl.program_id(0) == 0)
    def _():
        wob_ref[...] = wo_ref[...].astype(BF16)

    segments = [(k_ref[...].astype(BF16), v_ref[...].astype(BF16), None)]
    o = _gqa_attention(q_ref[...].astype(BF16), segments, sink_ref, CTX_LEN)
    m_out = jnp.dot(o.astype(BF16), wob_ref[...], preferred_element_type=F32)
    o_ref[...] = y_ref[...] + ada_ref[2:3, :] * m_out


def _attn_ctx_call(q, k, v, sink, wo, y_ctx, ada, layer, j):
    return pl.pallas_call(
        _attn_ctx_kernel,
        out_shape=STREAM_SHAPES[0],
        grid=(N_CTX_SEQ,),
        in_specs=[
            _tile_spec(), _tile_spec(KV_WIDTH), _tile_spec(KV_WIDTH),
            _layer_spec(j, (1, N_HEADS)),
            _layer_spec(j, (D, D)),
            _tile_spec(),
            _ada_spec(layer),
        ],
        out_specs=_tile_spec(),
        scratch_shapes=[pltpu.VMEM((D, D), BF16)],
        compiler_params=_params(("arbitrary",)),
        name="attn_ctx",
    )(q, k, v, sink.reshape(-1, 1, N_HEADS), wo, y_ctx, ada)


LAT_WIN = 3 * WINDOW
N_QBLOCKS = LAT_LEN // Q_BLOCK


def _attn_lat_kernel(q_ref, k_ref, v_ref, kc_ref, vc_ref, sink_ref, wo_ref, y_ref, ada_ref,
                     o_ref, wob_ref):
    b = pl.program_id(0)
    n = pl.program_id(1)

    @pl.when((b == 0) & (n == 0))
    def _():
        wob_ref[...] = wo_ref[...].astype(BF16)

    start = pl.multiple_of(jnp.clip((n - 1) * WINDOW, 0, LAT_LEN - LAT_WIN), WINDOW)
    kw = k_ref[pl.ds(start, LAT_WIN), :].astype(BF16)
    vw = v_ref[pl.ds(start, LAT_WIN), :].astype(BF16)
    qpos = n * Q_BLOCK + lax.broadcasted_iota(I32, (GQA * Q_BLOCK, LAT_WIN), 0) % Q_BLOCK
    kpos = start + lax.broadcasted_iota(I32, (GQA * Q_BLOCK, LAT_WIN), 1)
    band = jnp.where(jnp.abs(kpos - qpos) <= WINDOW, 0.0, -jnp.inf)
    segments = [(kw, vw, band),
                (kc_ref[...].astype(BF16), vc_ref[...].astype(BF16), None)]
    o = _gqa_attention(q_ref[...].astype(BF16), segments, sink_ref, Q_BLOCK)
    m_out = jnp.dot(o.astype(BF16), wob_ref[...], preferred_element_type=F32)
    o_ref[...] = y_ref[...] + ada_ref[2:3, :] * m_out


def _attn_lat_call(q, k, v, cache_k, cache_v, sink, wo, y_lat, ada, layer, j):
    q_row0 = TP // Q_BLOCK
    seq0 = TP // LAT_LEN
    qspec = pl.BlockSpec((Q_BLOCK, D), lambda b, n: (q_row0 + b * N_QBLOCKS + n, 0))
    yspec = pl.BlockSpec((Q_BLOCK, D), lambda b, n: (b * N_QBLOCKS + n, 0))
    kvspec = pl.BlockSpec((LAT_LEN, KV_WIDTH), lambda b, n: (seq0 + b, 0))
    cspec = pl.BlockSpec((None, None, CTX_LEN, KV_WIDTH), lambda b, n: (b, j, 0, 0))
    ck = cache_k.reshape(N_LAT_SEQ, -1, CTX_LEN, KV_WIDTH)
    cv = cache_v.reshape(N_LAT_SEQ, -1, CTX_LEN, KV_WIDTH)
    return pl.pallas_call(
        _attn_lat_kernel,
        out_shape=STREAM_SHAPES[1],
        grid=(N_LAT_SEQ, N_QBLOCKS),
        in_specs=[
            qspec, kvspec, kvspec, cspec, cspec,
            _layer_spec(j, (1, N_HEADS)),
            _layer_spec(j, (D, D)),
            yspec,
            pl.BlockSpec((None, None, N_ADA, D), lambda b, n: (layer, 1 + b, 0, 0)),
        ],
        out_specs=yspec,
        scratch_shapes=[pltpu.VMEM((D, D), BF16)],
        compiler_params=_params(("arbitrary", "arbitrary")),
        name="attn_lat",
    )(q, k, v, ck, cv, sink.reshape(-1, 1, N_HEADS), wo, y_lat, ada)


def _router_kernel(yc_ref, yl_ref, ada_ref, g_ref, rw_ref, rb_ref, sg_ref, su_ref, sd_ref,
                   h_ref, base_ref, key_ref, w_ref, cnt_ref,
                   carry_ref, tri_ref, sgb_ref, sub_ref, sdb_ref):
    i = pl.program_id(0)

    @pl.when(i == 0)
    def _():
        sgb_ref[...] = sg_ref[...].astype(BF16)
        sub_ref[...] = su_ref[...].astype(BF16)
        sdb_ref[...] = sd_ref[...].astype(BF16)
        r = lax.broadcasted_iota(I32, (ROW_TILE, ROW_TILE), 0)
        c = lax.broadcasted_iota(I32, (ROW_TILE, ROW_TILE), 1)
        tri_ref[...] = jnp.where(r < c, 1.0, 0.0).astype(BF16)

    @pl.when(i % TILES_PER_GROUP == 0)
    def _():
        carry_ref[...] = jnp.zeros_like(carry_ref)

    y = _read_stream(yc_ref, yl_ref)
    h = _modulate(y, g_ref[...], ada_ref[3:4, :], ada_ref[4:5, :])
    for c in range(LANE_CHUNKS):
        h_ref[pl.ds(c, ROW_TILE, stride=LANE_CHUNKS), :] = h[:, c * LANES:(c + 1) * LANES]
    hb = h.astype(BF16)

    d = functools.partial(jnp.dot, preferred_element_type=F32)
    act = (_silu(d(hb, sgb_ref[...])) * d(hb, sub_ref[...])).astype(BF16)
    base_ref[...] = y + ada_ref[5:6, :] * d(act, sdb_ref[...])

    logits = _dot3(h, rw_ref[...])
    logits_t = jnp.concatenate([logits, jnp.zeros_like(logits)], axis=1).T[:N_EXPERTS]
    scores = jax.nn.sigmoid(logits_t)
    shape3 = (N_EGROUPS, EGROUP, ROW_TILE)
    scores3 = scores.reshape(shape3)
    choice3 = (scores + rb_ref[...]).reshape(shape3)
    sub = lax.broadcasted_iota(I32, shape3, 1)
    eidx = lax.broadcasted_iota(I32, shape3, 0) * EGROUP + sub
    gidx = lax.broadcasted_iota(I32, (N_EGROUPS, 1, ROW_TILE), 0)
    neg = -jnp.inf

    m1 = jnp.max(choice3, axis=1, keepdims=True)
    i1 = jnp.min(jnp.where(choice3 == m1, sub, EGROUP), axis=1, keepdims=True)
    m2 = jnp.max(jnp.where(sub == i1, neg, choice3), axis=1, keepdims=True)
    gscore = m1 + m2
    allowed = jnp.zeros((N_EGROUPS, 1, ROW_TILE), jnp.bool_)
    for _ in range(TOPK_GROUPS):
        gm = jnp.max(gscore, axis=0, keepdims=True)
        gi = jnp.min(jnp.where(gscore == gm, gidx, N_EGROUPS), axis=0, keepdims=True)
        hit = gidx == gi
        allowed = allowed | hit
        gscore = jnp.where(hit, neg, gscore)

    def reduce_experts(fn, x):
        return fn(fn(x, axis=0, keepdims=True), axis=1, keepdims=True)

    cm = jnp.where(allowed, choice3, neg)
    picked = jnp.zeros(shape3, F32)
    e_rows, s_rows = [], []
    for _ in range(TOP_K):
        m = reduce_experts(jnp.max, cm)
        ik = reduce_experts(jnp.min, jnp.where(cm == m, eidx, N_EXPERTS))
        hit = eidx == ik
        e_rows.append(ik)
        s_rows.append(reduce_experts(jnp.sum, jnp.where(hit, scores3, 0.0)))
        cm = jnp.where(hit, neg, cm)
        picked = jnp.where(hit, 1.0, picked)

    picked2 = picked.reshape(N_EXPERTS, ROW_TILE)
    cum3 = (d(picked2.astype(BF16), tri_ref[...]) + carry_ref[...]).reshape(shape3)
    carry_ref[...] = carry_ref[...] + jnp.sum(picked2, axis=1, keepdims=True)
    cnt_ref[...] = jnp.broadcast_to(carry_ref[...], (N_EXPERTS, LANES))

    denom = functools.reduce(lambda a, b: a + b, s_rows)
    key_rows, w_rows = [], []
    for kk in range(TOP_K):
        rank = reduce_experts(jnp.sum, jnp.where(eidx == e_rows[kk], cum3, 0.0)).astype(I32)
        key_rows.append(((e_rows[kk] << RANK_BITS) | rank).reshape(1, ROW_TILE))
        w_rows.append((s_rows[kk] / denom * ROUTED_SCALE).reshape(1, ROW_TILE))
    key_ref[...] = jnp.concatenate(key_rows, axis=0)
    w_ref[...] = jnp.concatenate(w_rows, axis=0)


def _router_call(y, ada, norm_ffn, router_w, router_bias, sg, su, sd, layer):
    return pl.pallas_call(
        _router_kernel,
        out_shape=(
            jax.ShapeDtypeStruct((N_GROUPS, MOE_GROUP * LANE_CHUNKS, LANES), F32),
            jax.ShapeDtypeStruct((T, D), F32),
            jax.ShapeDtypeStruct((TOP_K, T), I32),
            jax.ShapeDtypeStruct((TOP_K, T), F32),
            jax.ShapeDtypeStruct((N_GROUPS, N_EXPERTS, LANES), F32),
        ),
        grid=(N_TILES,),
        in_specs=STREAM_SPECS + [
            _ada_spec(layer),
            _layer_spec(layer, (1, D)),
            _layer_spec(layer, (D, N_EXPERTS)),
            _layer_spec(layer, (N_EXPERTS, 1)),
            _layer_spec(layer, (D, D_SHARED)),
            _layer_spec(layer, (D, D_SHARED)),
            _layer_spec(layer, (D_SHARED, D)),
        ],
        out_specs=(
            pl.BlockSpec((None, ROW_TILE * LANE_CHUNKS, LANES),
                         lambda i: (i // TILES_PER_GROUP, i % TILES_PER_GROUP, 0)),
            _tile_spec(),
            pl.BlockSpec((TOP_K, ROW_TILE), lambda i: (0, i)),
            pl.BlockSpec((TOP_K, ROW_TILE), lambda i: (0, i)),
            pl.BlockSpec((None, N_EXPERTS, LANES), lambda i: (i // TILES_PER_GROUP, 0, 0)),
        ),
        scratch_shapes=[
            pltpu.VMEM((N_EXPERTS, 1), F32),
            pltpu.VMEM((ROW_TILE, ROW_TILE), BF16),
            pltpu.VMEM((D, D_SHARED), BF16),
            pltpu.VMEM((D, D_SHARED), BF16),
            pltpu.VMEM((D_SHARED, D), BF16),
        ],
        compiler_params=_params(("arbitrary",)),
        name="router",
    )(*y, ada, norm_ffn.reshape(DEPTH, 1, D), router_w, router_bias.reshape(DEPTH, N_EXPERTS, 1),
      sg, su, sd)


SLOT_WORDS = (MOE_NB + 2) * MOE_BM
SLOT_ROWS = SLOT_WORDS // LANES


def _pos_kernel(start_ref, key_ref, pos_ref):
    g = pl.program_id(0)
    key = key_ref[...]
    e = key >> RANK_BITS
    pos = key & ((1 << RANK_BITS) - 1)
    for ee in range(N_EXPERTS):
        pos = pos + jnp.where(e == ee, start_ref[g * N_EXPERTS + ee], 0)
    pos_ref[...] = pos


def _pos_call(starts, keys):
    spec = pl.BlockSpec((TOP_K, MOE_GROUP), lambda g, s: (0, g))
    return pl.pallas_call(
        _pos_kernel,
        out_shape=jax.ShapeDtypeStruct((TOP_K, T), I32),
        grid_spec=pltpu.PrefetchScalarGridSpec(
            num_scalar_prefetch=1, grid=(N_GROUPS,), in_specs=[spec], out_specs=spec),
        compiler_params=_params(("arbitrary",)),
        name="moe_pos",
    )(starts, keys)


SLOT_ROW_BITS = 16
DUMMY_SLOT = MOE_GROUP * SUBLANES
SLOT_UNROLL = 4


def _slot_kernel(pos_ref, init_hbm, slot_ref, sem):
    fill = pltpu.make_async_copy(init_hbm, slot_ref, sem)
    fill.start()
    fill.wait()
    per_token = SUBLANES + (1 << SLOT_ROW_BITS)

    def place(t, carry):
        word = t * per_token
        for k in range(TOP_K):
            slot_ref[pos_ref[k * MOE_GROUP + t]] = word + ((k * MOE_GROUP) << SLOT_ROW_BITS)
        return carry

    lax.fori_loop(0, MOE_GROUP, place, 0, unroll=SLOT_UNROLL)


def _slot_call(pos):
    smem = pl.BlockSpec(memory_space=pltpu.SMEM)
    return pl.pallas_call(
        _slot_kernel,
        out_shape=jax.ShapeDtypeStruct((SLOT_WORDS,), I32),
        in_specs=[smem, pl.BlockSpec(memory_space=pl.ANY)],
        out_specs=smem,
        scratch_shapes=[pltpu.SemaphoreType.DMA],
        name="moe_slots",
    )(pos, jnp.full((SLOT_WORDS,), DUMMY_SLOT, I32))


TILE_ROWS = MOE_GROUP * SUBLANES
ACC_TILE_ROWS = ACC_ROWS * SUBLANES
RMW_ROWS = 16


def _expert_kernel(blk_exp_ref, n_used_ref,
                   slot_hbm, w_hbm, x_hbm, wg_ref, wu_ref, wd_ref,
                   out_hbm,
                   x_vmem, acc_vmem, gat0_ref, gat1_ref, yt0_ref, yt1_ref,
                   wgb_ref, wub_ref, wdb_ref, slot_smem, w_smem, sems):
    g = pl.program_id(0)
    j = pl.program_id(1)
    idx = g * MOE_NB + jnp.minimum(j, MOE_NB - 1)
    rows_per_block = MOE_BM // LANES

    def slot_word(block, r):
        return slot_smem[block * rows_per_block + r // LANES, r % LANES]

    def tile_rows(word):
        return pl.ds(pl.multiple_of(word & ((1 << SLOT_ROW_BITS) - 1), SUBLANES), SUBLANES)

    def gather(block, gat_ref):
        for r in range(MOE_BM):
            gat_ref[pl.ds(r * SUBLANES, SUBLANES), :] = x_vmem[tile_rows(slot_word(block, r)), :]

    def expert_mlp(gat_ref, yt_ref):
        xs = jnp.concatenate(
            [gat_ref[pl.ds(c, MOE_BM, stride=LANE_CHUNKS), :] for c in range(LANE_CHUNKS)],
            axis=1).astype(BF16)
        d = functools.partial(jnp.dot, preferred_element_type=F32)
        act = (_silu(d(xs, wgb_ref[...])) * d(xs, wub_ref[...])).astype(BF16)
        yb = d(act, wdb_ref[...])
        for c in range(LANE_CHUNKS):
            yt_ref[pl.ds(c, MOE_BM, stride=LANE_CHUNKS), :] = yb[:, c * LANES:(c + 1) * LANES]

    def combine(block, yt_ref):
        for r0 in range(0, MOE_BM, RMW_ROWS):
            words = [slot_word(block, r) for r in range(r0, r0 + RMW_ROWS)]
            new = [acc_vmem[tile_rows(word), :]
                   + w_smem[word >> SLOT_ROW_BITS] * yt_ref[pl.ds(r * SUBLANES, SUBLANES), :]
                   for r, word in zip(range(r0, r0 + RMW_ROWS), words)]
            for word, tile in zip(words, new):
                acc_vmem[tile_rows(word), :] = tile

    def group_copies():
        return (pltpu.make_async_copy(x_hbm.at[g], x_vmem.at[pl.ds(0, TILE_ROWS)], sems.at[0]),
                pltpu.make_async_copy(slot_hbm.at[g], slot_smem, sems.at[1]),
                pltpu.make_async_copy(w_hbm.at[g], w_smem, sems.at[2]))

    @pl.when(j == 0)
    def _():
        for c in group_copies():
            c.start()
        acc_vmem[...] = jnp.zeros_like(acc_vmem)
        x_vmem[pl.ds(TILE_ROWS, ACC_TILE_ROWS - TILE_ROWS), :] = jnp.zeros(
            (ACC_TILE_ROWS - TILE_ROWS, LANES), F32)
        yt1_ref[...] = jnp.zeros_like(yt1_ref)
        for c in group_copies():
            c.wait()
        gather(0, gat0_ref)

    e_cur = blk_exp_ref[idx]
    e_prev = blk_exp_ref[jnp.maximum(idx - 1, 0)]

    @pl.when((j == 0) | (e_cur != e_prev))
    def _():
        wgb_ref[...] = wg_ref[...].astype(BF16)
        wub_ref[...] = wu_ref[...].astype(BF16)
        wdb_ref[...] = wd_ref[...].astype(BF16)

    active = j <= n_used_ref[g]
    prev_block = jnp.maximum(j - 1, 0)

    @pl.when(active & (j % 2 == 0))
    def _():
        gather(j + 1, gat1_ref)
        expert_mlp(gat0_ref, yt0_ref)
        combine(prev_block, yt1_ref)

    @pl.when(active & (j % 2 == 1))
    def _():
        gather(j + 1, gat0_ref)
        expert_mlp(gat1_ref, yt1_ref)
        combine(prev_block, yt0_ref)

    @pl.when(j == MOE_NB)
    def _():
        out_copy = pltpu.make_async_copy(acc_vmem, out_hbm.at[g], sems.at[3])
        out_copy.start()
        out_copy.wait()


def _expert_call(blk_exp, n_used, slots, ws, x_tiles, wg, wu, wd, layer):
    def wspec(shape):
        return pl.BlockSpec((None, None) + shape,
                            lambda g, j, be, nu: (layer, be[g * MOE_NB + jnp.minimum(j, MOE_NB - 1)], 0, 0))

    any_spec = pl.BlockSpec(memory_space=pl.ANY)
    return pl.pallas_call(
        _expert_kernel,
        out_shape=jax.ShapeDtypeStruct((N_GROUPS, ACC_TILE_ROWS, LANES), F32),
        grid_spec=pltpu.PrefetchScalarGridSpec(
            num_scalar_prefetch=2,
            grid=(N_GROUPS, MOE_NB + 1),
            in_specs=[any_spec, any_spec, any_spec,
                      wspec((D, D_EXPERT)), wspec((D, D_EXPERT)), wspec((D_EXPERT, D))],
            out_specs=any_spec,
            scratch_shapes=[
                pltpu.VMEM((ACC_TILE_ROWS, LANES), F32),
                pltpu.VMEM((ACC_TILE_ROWS, LANES), F32),
                pltpu.VMEM((MOE_BM * SUBLANES, LANES), F32),
                pltpu.VMEM((MOE_BM * SUBLANES, LANES), F32),
                pltpu.VMEM((MOE_BM * SUBLANES, LANES), F32),
                pltpu.VMEM((MOE_BM * SUBLANES, LANES), F32),
                pltpu.VMEM((D, D_EXPERT), BF16),
                pltpu.VMEM((D, D_EXPERT), BF16),
                pltpu.VMEM((D_EXPERT, D), BF16),
                pltpu.SMEM((SLOT_ROWS, LANES), I32),
                pltpu.SMEM((N_ASSIGN,), F32),
                pltpu.SemaphoreType.DMA((4,)),
            ]),
        compiler_params=_params(("arbitrary", "arbitrary")),
        name="moe_experts",
    )(blk_exp, n_used, slots, ws, x_tiles, wg, wu, wd)


def _finish_kernel(base_ref, routed_ref, ada_ref, oc_ref, ol_ref):
    routed = jnp.concatenate(
        [routed_ref[pl.ds(c, ROW_TILE, stride=LANE_CHUNKS), :] for c in range(LANE_CHUNKS)], axis=1)
    _write_stream(oc_ref, ol_ref, base_ref[...] + ada_ref[5:6, :] * routed)


def _finish_call(base, routed, ada, layer):
    rows = ROW_TILE * LANE_CHUNKS
    return pl.pallas_call(
        _finish_kernel,
        out_shape=STREAM_SHAPES,
        grid=(N_TILES,),
        in_specs=[
            _tile_spec(),
            pl.BlockSpec((None, rows, LANES), lambda i: (i // TILES_PER_GROUP, i % TILES_PER_GROUP, 0)),
            _ada_spec(layer),
        ],
        out_specs=STREAM_SPECS,
        compiler_params=_params(("arbitrary",)),
        name="moe_finish",
    )(base, routed, ada)


BM_SHIFT = MOE_BM.bit_length() - 1


def _plan_kernel(cnt_ref, start_ref, n_used_ref, blk_exp_ref):
    for g in range(N_GROUPS):
        def expert(e, run):
            blocks = (cnt_ref[g * N_EXPERTS + e] + MOE_BM - 1) >> BM_SHIFT
            start_ref[g * N_EXPERTS + e] = run << BM_SHIFT

            def mark(b, c):
                blk_exp_ref[g * MOE_NB + b] = e
                return c

            lax.fori_loop(run, run + blocks, mark, 0)
            return run + blocks

        used = lax.fori_loop(0, N_EXPERTS, expert, 0)
        n_used_ref[g] = used

        def mark_unused(b, c):
            blk_exp_ref[g * MOE_NB + b] = N_EXPERTS - 1
            return c

        lax.fori_loop(used, MOE_NB, mark_unused, 0)


def _plan_call(counts):
    smem = pl.BlockSpec(memory_space=pltpu.SMEM)
    return pl.pallas_call(
        _plan_kernel,
        out_shape=(jax.ShapeDtypeStruct((N_GROUPS * N_EXPERTS,), I32),
                   jax.ShapeDtypeStruct((N_GROUPS,), I32),
                   jax.ShapeDtypeStruct((N_GROUPS * MOE_NB,), I32)),
        in_specs=[smem],
        out_specs=(smem, smem, smem),
        name="moe_plan",
    )(counts)


def _moe_layer(y, ada, norm_ffn, router_w, router_bias, wg, wu, wd, sg, su, sd, layer):
    x_tiles, base, keys, ws, cnts = _router_call(
        y, ada, norm_ffn, router_w, router_bias, sg, su, sd, layer)
    starts, n_used, blk_exp = _plan_call(cnts[:, :, 0].astype(I32).reshape(-1))
    pos = _pos_call(starts, keys)
    slots = jnp.stack([
        _slot_call(pos[:, g * MOE_GROUP:(g + 1) * MOE_GROUP].reshape(-1))
        for g in range(N_GROUPS)]).reshape(N_GROUPS, SLOT_ROWS, LANES)
    ws_g = ws.reshape(TOP_K, N_GROUPS, MOE_GROUP).transpose(1, 0, 2).reshape(N_GROUPS, N_ASSIGN)
    routed = _expert_call(blk_exp, n_used, slots, ws_g, x_tiles, wg, wu, wd, layer)
    return _finish_call(base, routed, ada, layer)


def kernel(x_prompt, x_sample, cache_k, cache_v, c, c_ctx, ada_w, ada_b, norm_mix, norm_ffn,
           conv_w1, conv_b1, conv_dw, conv_dw_b, conv_norm, conv_w2, conv_b2,
           attn_wqkv, attn_q_norm, attn_k_norm, attn_sink, attn_wo,
           router_w, router_bias, exp_w_gate, exp_w_up, exp_w_down,
           sh_w_gate, sh_w_up, sh_w_down):
    y = (x_prompt.reshape(TP, D), x_sample.reshape(TS, D))
    cond = jnp.concatenate(
        [c_ctx[None, :], c, jnp.zeros((COND_ROWS - N_COND, D), F32)], axis=0)
    ada = _ada_call(cond, ada_w, ada_b).reshape(DEPTH, COND_ROWS, N_ADA, D)

    new_k = new_v = None
    for layer in range(DEPTH):
        j = layer // 2
        if layer % 2 == 0:
            u = _conv_in_call(y, ada, norm_mix, conv_w1, conv_b1, layer, j)
            y = _conv_out_call(u, y, ada, conv_dw, conv_dw_b, conv_norm, conv_w2, conv_b2, layer, j)
        else:
            q, k, v = _qkv_call(y, ada, norm_mix, attn_wqkv, attn_q_norm, attn_k_norm, layer, j)
            new_k = k[:TP].reshape(N_CTX_SEQ, 1, CTX_LEN, N_KV, HEAD_DIM)
            new_v = v[:TP].reshape(N_CTX_SEQ, 1, CTX_LEN, N_KV, HEAD_DIM)
            y = (_attn_ctx_call(q, k, v, attn_sink, attn_wo, y[0], ada, layer, j),
                 _attn_lat_call(q, k, v, cache_k, cache_v, attn_sink, attn_wo, y[1], ada, layer, j))
        y = _moe_layer(y, ada, norm_ffn, router_w, router_bias, exp_w_gate, exp_w_up, exp_w_down,
                       sh_w_gate, sh_w_up, sh_w_down, layer)

    y_p = y[0].reshape(N_CTX_SEQ, CTX_LEN, D)
    y_s = y[1].reshape(N_LAT_SEQ, LAT_LEN, D)
    return (y_p, y_s, new_k, new_v)
```

```python
import functools

import jax
import jax.numpy as jnp
import numpy as np
from jax import lax
from jax.experimental import pallas as pl
from jax.experimental.pallas import tpu as pltpu

F32 = jnp.float32
BF16 = jnp.bfloat16
I32 = jnp.int32

D = 1024
N_CTX_SEQ = 32
CTX_LEN = 256
N_LAT_SEQ = 4
LAT_LEN = 1024
TP = N_CTX_SEQ * CTX_LEN
TS = N_LAT_SEQ * LAT_LEN
T = TP + TS
DEPTH = 2
N_ADA = 6
N_COND = 1 + N_LAT_SEQ
COND_ROWS = 8
CONV_WIDTH = 31
CONV_PAD = CONV_WIDTH // 2
N_HEADS = 16
N_KV = 4
HEAD_DIM = 64
GQA = N_HEADS // N_KV
KV_WIDTH = N_KV * HEAD_DIM
QKV_WIDTH = D + 2 * KV_WIDTH
WINDOW = 128
GRID_W = 64
ROPE_PAIRS = HEAD_DIM // 4
ROPE_THETA = 10000.0
N_EXPERTS = 64
N_EGROUPS = 8
EGROUP = N_EXPERTS // N_EGROUPS
TOPK_GROUPS = 4
TOP_K = 8
D_EXPERT = 256
D_SHARED = 256
ROUTED_SCALE = 2.5
NORM_EPS = 1e-6

LANES = 128
SUBLANES = 8
VMEM_LIMIT = 56 * 1024 * 1024

ROW_TILE = 256
N_TILES = T // ROW_TILE
N_CTX_TILES = TP // ROW_TILE
LAT_TILES_PER_SEQ = LAT_LEN // ROW_TILE
HALO = 16
LANE_CHUNKS = D // LANES

MOE_GROUP = 4096
N_GROUPS = T // MOE_GROUP
TILES_PER_GROUP = MOE_GROUP // ROW_TILE
MOE_BM = 256
MOE_NB = MOE_GROUP * TOP_K // MOE_BM + N_EXPERTS
N_ASSIGN = MOE_GROUP * TOP_K
ACC_ROWS = MOE_GROUP + SUBLANES
RANK_BITS = 12
Q_BLOCK = 128


def _cond_of_tile(i):
    return jnp.where(i < N_CTX_TILES, 0, 1 + (i - N_CTX_TILES) // LAT_TILES_PER_SEQ)


def _params(sem, vmem=VMEM_LIMIT):
    return pltpu.CompilerParams(dimension_semantics=sem, vmem_limit_bytes=vmem)


def _bdot(a, b):
    return jnp.dot(a.astype(BF16), b.astype(BF16), preferred_element_type=F32)


def _split(a):
    hi = a.astype(BF16)
    lo = (a - hi.astype(F32)).astype(BF16)
    return hi, lo


def _dot3(a, b):
    a_hi, a_lo = _split(a)
    b_hi, b_lo = _split(b)
    d = functools.partial(jnp.dot, preferred_element_type=F32)
    return d(a_hi, b_hi) + d(a_lo, b_hi) + d(a_hi, b_lo)


def _dot2(a, b_bf16):
    a_hi, a_lo = _split(a)
    d = functools.partial(jnp.dot, preferred_element_type=F32)
    return d(a_hi, b_bf16) + d(a_lo, b_bf16)


def _rms(x, g):
    return x * lax.rsqrt(jnp.mean(x * x, axis=-1, keepdims=True) + NORM_EPS) * g


def _modulate(x, g, shift, scale):
    return _rms(x, g) * (1.0 + scale) + shift


def _silu(x):
    return x * jax.nn.sigmoid(x)


def _ada_spec(layer):
    return pl.BlockSpec((None, None, N_ADA, D), lambda i: (layer, _cond_of_tile(i), 0, 0))


def _tile_spec(width=D):
    return pl.BlockSpec((ROW_TILE, width), lambda i: (i, 0))


STREAM_SPECS = [
    pl.BlockSpec((ROW_TILE, D), lambda i: (jnp.minimum(i, N_CTX_TILES - 1), 0)),
    pl.BlockSpec((ROW_TILE, D), lambda i: (jnp.maximum(i - N_CTX_TILES, 0), 0)),
]
STREAM_SHAPES = (jax.ShapeDtypeStruct((TP, D), F32), jax.ShapeDtypeStruct((TS, D), F32))


def _read_stream(ctx_ref, lat_ref):
    return jnp.where(pl.program_id(0) < N_CTX_TILES, ctx_ref[...], lat_ref[...])


def _write_stream(ctx_ref, lat_ref, value):
    i = pl.program_id(0)

    @pl.when(i < N_CTX_TILES)
    def _():
        ctx_ref[...] = value

    @pl.when(i >= N_CTX_TILES)
    def _():
        lat_ref[...] = value


def _const_spec(shape):
    nd = len(shape)
    return pl.BlockSpec(shape, lambda *_: (0,) * nd)


def _layer_spec(layer, shape):
    nd = len(shape)
    return pl.BlockSpec((None,) + tuple(shape), lambda *_: (layer,) + (0,) * nd)


ADA_NB = 512


def _ada_kernel(c_ref, w_ref, b_ref, o_ref):
    o_ref[...] = _dot3(_silu(c_ref[...]), w_ref[...]) + b_ref[...]


def _ada_call(cond, ada_w, ada_b):
    return pl.pallas_call(
        _ada_kernel,
        out_shape=jax.ShapeDtypeStruct((DEPTH, COND_ROWS, N_ADA * D), F32),
        grid=(DEPTH, N_ADA * D // ADA_NB),
        in_specs=[
            pl.BlockSpec((COND_ROWS, D), lambda l, n: (0, 0)),
            pl.BlockSpec((None, D, ADA_NB), lambda l, n: (l, 0, n)),
            pl.BlockSpec((None, 1, ADA_NB), lambda l, n: (l, 0, n)),
        ],
        out_specs=pl.BlockSpec((None, COND_ROWS, ADA_NB), lambda l, n: (l, 0, n)),
        compiler_params=_params(("arbitrary", "arbitrary")),
        name="ada_params",
    )(cond, ada_w, ada_b.reshape(DEPTH, 1, N_ADA * D))


def _conv_in_kernel(xc_ref, xl_ref, ada_ref, g_ref, w1_ref, b1_ref, u_ref, w1b_ref):
    @pl.when(pl.program_id(0) == 0)
    def _():
        w1b_ref[...] = w1_ref[...].astype(BF16)

    h = _modulate(_read_stream(xc_ref, xl_ref), g_ref[...], ada_ref[0:1, :], ada_ref[1:2, :])
    u = jnp.dot(h.astype(BF16), w1b_ref[...], preferred_element_type=F32) + b1_ref[...]
    u_ref[...] = u[:, :D] * jax.nn.sigmoid(u[:, D:])


def _conv_in_call(y, ada, norm_mix, conv_w1, conv_b1, layer, j):
    return pl.pallas_call(
        _conv_in_kernel,
        out_shape=jax.ShapeDtypeStruct((T, D), F32),
        grid=(N_TILES,),
        in_specs=STREAM_SPECS + [
            _ada_spec(layer),
            _layer_spec(layer, (1, D)),
            _layer_spec(j, (D, 2 * D)),
            _layer_spec(j, (1, 2 * D)),
        ],
        out_specs=_tile_spec(),
        scratch_shapes=[pltpu.VMEM((D, 2 * D), BF16)],
        compiler_params=_params(("arbitrary",)),
        name="conv_in",
    )(*y, ada, norm_mix.reshape(DEPTH, 1, D), conv_w1, conv_b1.reshape(-1, 1, 2 * D))


CONV_ROWS = 64
BUF_ROWS = ROW_TILE + 2 * HALO


def _conv_out_kernel(u_ref, up_ref, un_ref, yc_ref, yl_ref, ada_ref, dw_ref, dwb_ref, gn_ref,
                     w2_ref, b2_ref, oc_ref, ol_ref, buf_ref, z_ref, w2b_ref):
    i = pl.program_id(0)

    @pl.when(i == 0)
    def _():
        w2b_ref[...] = w2_ref[...].astype(BF16)

    lat = i >= N_CTX_TILES
    pos = jnp.where(lat, i - N_CTX_TILES, 0) % LAT_TILES_PER_SEQ
    has_prev = lat & (pos != 0)
    has_next = lat & (pos != LAT_TILES_PER_SEQ - 1)
    for c in range(LANE_CHUNKS):
        cs = slice(c * LANES, (c + 1) * LANES)
        buf_ref[c, 0:HALO, :] = jnp.where(has_prev, up_ref[:, cs], 0.0)
        buf_ref[c, HALO:HALO + ROW_TILE, :] = u_ref[:, cs]
        buf_ref[c, HALO + ROW_TILE:BUF_ROWS, :] = jnp.where(has_next, un_ref[:, cs], 0.0)

    off = HALO - CONV_PAD
    for c in range(LANE_CHUNKS):
        cs = slice(c * LANES, (c + 1) * LANES)
        for r0 in range(0, ROW_TILE, CONV_ROWS):
            acc = jnp.broadcast_to(dwb_ref[:, cs], (CONV_ROWS, LANES))
            for k in range(CONV_WIDTH):
                win = buf_ref[c, r0 + k + off:r0 + k + off + CONV_ROWS, :]
                acc = acc + dw_ref[k:k + 1, cs] * win
            z_ref[r0:r0 + CONV_ROWS, cs] = acc

    z = _silu(_rms(z_ref[...], gn_ref[...]))
    m = jnp.dot(z.astype(BF16), w2b_ref[...], preferred_element_type=F32) + b2_ref[...]
    _write_stream(oc_ref, ol_ref, _read_stream(yc_ref, yl_ref) + ada_ref[2:3, :] * m)


def _conv_out_call(u, y, ada, conv_dw, conv_dw_b, conv_norm, conv_w2, conv_b2, layer, j):
    halos_per_tile = ROW_TILE // HALO
    last_halo = T // HALO - 1
    return pl.pallas_call(
        _conv_out_kernel,
        out_shape=STREAM_SHAPES,
        grid=(N_TILES,),
        in_specs=[
            _tile_spec(),
            pl.BlockSpec((HALO, D), lambda i: (jnp.maximum(i * halos_per_tile - 1, 0), 0)),
            pl.BlockSpec((HALO, D), lambda i: (jnp.minimum((i + 1) * halos_per_tile, last_halo), 0)),
        ] + STREAM_SPECS + [
            _ada_spec(layer),
            _layer_spec(j, (CONV_WIDTH + 1, D)),
            _layer_spec(j, (1, D)),
            _layer_spec(j, (1, D)),
            _layer_spec(j, (D, D)),
            _layer_spec(j, (1, D)),
        ],
        out_specs=STREAM_SPECS,
        scratch_shapes=[pltpu.VMEM((LANE_CHUNKS, BUF_ROWS, LANES), F32), pltpu.VMEM((ROW_TILE, D), F32),
                        pltpu.VMEM((D, D), BF16)],
        compiler_params=_params(("arbitrary",)),
        name="conv_out",
    )(u, u, u, *y, ada, jnp.pad(conv_dw, ((0, 0), (0, 1), (0, 0))), conv_dw_b.reshape(-1, 1, D),
      conv_norm.reshape(-1, 1, D), conv_w2, conv_b2.reshape(-1, 1, D))


def _swap_pairs(x, width):
    lane = lax.broadcasted_iota(I32, x.shape, 1)
    first = (lane % (2 * ROPE_PAIRS)) < ROPE_PAIRS
    return jnp.where(first, pltpu.roll(x, width - ROPE_PAIRS, 1), pltpu.roll(x, ROPE_PAIRS, 1))


def _qkv_kernel(xc_ref, xl_ref, ada_ref, g_ref, w_ref, qn_ref, kn_ref, cos_ref, sin_ref,
                q_ref, k_ref, v_ref, wb_ref, hs_ref):
    i = pl.program_id(0)

    @pl.when(i == 0)
    def _():
        wb_ref[...] = w_ref[...].astype(BF16)
        r = lax.broadcasted_iota(I32, (D, D), 0) // HEAD_DIM
        c = lax.broadcasted_iota(I32, (D, D), 1) // HEAD_DIM
        hs_ref[...] = jnp.where(r == c, 1.0 / HEAD_DIM, 0.0).astype(BF16)

    h = _modulate(_read_stream(xc_ref, xl_ref), g_ref[...], ada_ref[0:1, :], ada_ref[1:2, :])
    qkv = jnp.dot(h.astype(BF16), wb_ref[...], preferred_element_type=F32)
    q = qkv[:, :D]
    k = qkv[:, D:D + KV_WIDTH]
    v_ref[...] = qkv[:, D + KV_WIDTH:]
    q_ms = _dot2(q * q, hs_ref[...])
    k_ms = _dot2(k * k, hs_ref[0:KV_WIDTH, 0:KV_WIDTH])
    qn = q * lax.rsqrt(q_ms + NORM_EPS) * qn_ref[...]
    kn = k * lax.rsqrt(k_ms + NORM_EPS) * kn_ref[...]

    @pl.when(i < N_CTX_TILES)
    def _():
        q_ref[...] = qn
        k_ref[...] = kn

    @pl.when(i >= N_CTX_TILES)
    def _():
        cos = cos_ref[...]
        sin = sin_ref[...]
        q_ref[...] = qn * cos + _swap_pairs(qn, D) * sin
        k_ref[...] = kn * cos[:, :KV_WIDTH] + _swap_pairs(kn, KV_WIDTH) * sin[:, :KV_WIDTH]


def _rope_tables():
    pos = np.arange(LAT_LEN)
    row = (pos // GRID_W).astype(np.float32)
    col = (pos % GRID_W).astype(np.float32)
    inv_freq = np.float32(ROPE_THETA) ** (-np.arange(ROPE_PAIRS, dtype=np.float32) / ROPE_PAIRS)
    d = np.arange(D) % HEAD_DIM
    freq = inv_freq[d % ROPE_PAIRS].astype(np.float32)
    p = np.where((d >= HEAD_DIM // 2)[None, :], col[:, None], row[:, None])
    ang = (p * freq[None, :]).astype(np.float32)
    first = (d % (2 * ROPE_PAIRS)) < ROPE_PAIRS
    cos = np.cos(ang).astype(np.float32)
    sin = np.sin(ang).astype(np.float32)
    return jnp.asarray(cos), jnp.asarray(np.where(first[None, :], -sin, sin))


def _qkv_call(y, ada, norm_mix, wqkv, q_norm, k_norm, layer, j):
    cos, sin = _rope_tables()
    table_spec = pl.BlockSpec(
        (ROW_TILE, D), lambda i: (jnp.maximum(i - N_CTX_TILES, 0) % LAT_TILES_PER_SEQ, 0))
    return pl.pallas_call(
        _qkv_kernel,
        out_shape=(jax.ShapeDtypeStruct((T, D), F32), jax.ShapeDtypeStruct((T, KV_WIDTH), F32),
                   jax.ShapeDtypeStruct((T, KV_WIDTH), F32)),
        grid=(N_TILES,),
        in_specs=STREAM_SPECS + [
            _ada_spec(layer),
            _layer_spec(layer, (1, D)),
            _layer_spec(j, (D, QKV_WIDTH)),
            _const_spec((1, D)),
            _const_spec((1, KV_WIDTH)),
            table_spec,
            table_spec,
        ],
        out_specs=(_tile_spec(), _tile_spec(KV_WIDTH), _tile_spec(KV_WIDTH)),
        scratch_shapes=[pltpu.VMEM((D, QKV_WIDTH), BF16), pltpu.VMEM((D, D), BF16)],
        compiler_params=_params(("arbitrary",)),
        name="qkv",
    )(*y, ada, norm_mix.reshape(DEPTH, 1, D), wqkv,
      jnp.tile(q_norm[j], N_HEADS).reshape(1, D), jnp.tile(k_norm[j], N_KV).reshape(1, KV_WIDTH),
      cos, sin)


HEADS_PER_TILE = LANES // HEAD_DIM
TILES_PER_KV = GQA // HEADS_PER_TILE


def _gqa_attention(q, segments, sink_ref, rows):
    d = functools.partial(jnp.dot, preferred_element_type=F32)
    half = lax.broadcasted_iota(I32, (rows, LANES), 1) < HEAD_DIM
    q_zero = jnp.zeros((rows, LANES), BF16)

    scores, sinks, kvs = [], [], []
    for kv in range(N_KV):
        sl = slice(kv * HEAD_DIM, (kv + 1) * HEAD_DIM)
        parts = []
        for t in range(TILES_PER_KV):
            tile = kv * TILES_PER_KV + t
            qt = q[:, tile * LANES:(tile + 1) * LANES]
            parts += [jnp.where(half, qt, q_zero), jnp.where(half, q_zero, qt)]
        qg = jnp.concatenate(parts, axis=0)
        seg_scores, seg_kv = [], []
        for k, v, bias in segments:
            kk = jnp.concatenate([k[:, sl], k[:, sl]], axis=1)
            s = lax.dot_general(qg, kk, (((1,), (1,)), ((), ())),
                                preferred_element_type=F32) * (HEAD_DIM ** -0.5)
            seg_scores.append(s if bias is None else s + bias)
            v_zero = jnp.zeros_like(v[:, sl])
            seg_kv.append((jnp.concatenate([v[:, sl], v_zero], axis=1),
                           jnp.concatenate([v_zero, v[:, sl]], axis=1)))
        scores.append(seg_scores)
        kvs.append(seg_kv)
        sinks.append(jnp.concatenate(
            [jnp.broadcast_to(sink_ref[0:1, kv * GQA + g:kv * GQA + g + 1], (rows, 1))
             for g in range(GQA)], axis=0))

    maxes = []
    for kv in range(N_KV):
        m = sinks[kv]
        for s in scores[kv]:
            m = jnp.maximum(m, jnp.max(s, axis=1, keepdims=True))
        maxes.append(m)
    probs = [[jnp.exp(s - maxes[kv]).astype(BF16) for s in scores[kv]] for kv in range(N_KV)]

    tiles = []
    for kv in range(N_KV):
        den = jnp.exp(sinks[kv] - maxes[kv])
        for p in probs[kv]:
            den = den + d(p, jnp.ones((p.shape[1], LANES), BF16))
        inv = 1.0 / den
        for t in range(TILES_PER_KV):
            tile = None
            for h in range(HEADS_PER_TILE):
                g = t * HEADS_PER_TILE + h
                rs = slice(g * rows, (g + 1) * rows)
                o = None
                for p, v_halves in zip(probs[kv], kvs[kv]):
                    part = d(p[rs, :], v_halves[h])
                    o = part if o is None else o + part
                o = o * inv[rs, :]
                tile = o if tile is None else tile + o
            tiles.append(tile)
    return jnp.concatenate(tiles, axis=1)


def _attn_ctx_kernel(q_ref, k_ref, v_ref, sink_ref, wo_ref, y_ref, ada_ref, o_ref, wob_ref):
    @pl.when(pl.program_id(0) == 0)
    def _():
        wob_ref[...] = wo_ref[...].astype(BF16)

    segments = [(k_ref[...].astype(BF16), v_ref[...].astype(BF16), None)]
    o = _gqa_attention(q_ref[...].astype(BF16), segments, sink_ref, CTX_LEN)
    m_out = jnp.dot(o.astype(BF16), wob_ref[...], preferred_element_type=F32)
    o_ref[...] = y_ref[...] + ada_ref[2:3, :] * m_out


def _attn_ctx_call(q, k, v, sink, wo, y_ctx, ada, layer, j):
    return pl.pallas_call(
        _attn_ctx_kernel,
        out_shape=STREAM_SHAPES[0],
        grid=(N_CTX_SEQ,),
        in_specs=[
            _tile_spec(), _tile_spec(KV_WIDTH), _tile_spec(KV_WIDTH),
            _layer_spec(j, (1, N_HEADS)),
            _layer_spec(j, (D, D)),
            _tile_spec(),
            _ada_spec(layer),
        ],
        out_specs=_tile_spec(),
        scratch_shapes=[pltpu.VMEM((D, D), BF16)],
        compiler_params=_params(("arbitrary",)),
        name="attn_ctx",
    )(q, k, v, sink.reshape(-1, 1, N_HEADS), wo, y_ctx, ada)


LAT_WIN = 3 * WINDOW
N_QBLOCKS = LAT_LEN // Q_BLOCK


def _attn_lat_kernel(q_ref, k_ref, v_ref, kc_ref, vc_ref, sink_ref, wo_ref, y_ref, ada_ref,
                     o_ref, wob_ref):
    b = pl.program_id(0)
    n = pl.program_id(1)

    @pl.when((b == 0) & (n == 0))
    def _():
        wob_ref[...] = wo_ref[...].astype(BF16)

    start = pl.multiple_of(jnp.clip((n - 1) * WINDOW, 0, LAT_LEN - LAT_WIN), WINDOW)
    kw = k_ref[pl.ds(start, LAT_WIN), :].astype(BF16)
    vw = v_ref[pl.ds(start, LAT_WIN), :].astype(BF16)
    qpos = n * Q_BLOCK + lax.broadcasted_iota(I32, (GQA * Q_BLOCK, LAT_WIN), 0) % Q_BLOCK
    kpos = start + lax.broadcasted_iota(I32, (GQA * Q_BLOCK, LAT_WIN), 1)
    band = jnp.where(jnp.abs(kpos - qpos) <= WINDOW, 0.0, -jnp.inf)
    segments = [(kw, vw, band),
                (kc_ref[...].astype(BF16), vc_ref[...].astype(BF16), None)]
    o = _gqa_attention(q_ref[...].astype(BF16), segments, sink_ref, Q_BLOCK)
    m_out = jnp.dot(o.astype(BF16), wob_ref[...], preferred_element_type=F32)
    o_ref[...] = y_ref[...] + ada_ref[2:3, :] * m_out


def _attn_lat_call(q, k, v, cache_k, cache_v, sink, wo, y_lat, ada, layer, j):
    q_row0 = TP // Q_BLOCK
    seq0 = TP // LAT_LEN
    qspec = pl.BlockSpec((Q_BLOCK, D), lambda b, n: (q_row0 + b * N_QBLOCKS + n, 0))
    yspec = pl.BlockSpec((Q_BLOCK, D), lambda b, n: (b * N_QBLOCKS + n, 0))
    kvspec = pl.BlockSpec((LAT_LEN, KV_WIDTH), lambda b, n: (seq0 + b, 0))
    cspec = pl.BlockSpec((None, None, CTX_LEN, KV_WIDTH), lambda b, n: (b, j, 0, 0))
    ck = cache_k.reshape(N_LAT_SEQ, -1, CTX_LEN, KV_WIDTH)
    cv = cache_v.reshape(N_LAT_SEQ, -1, CTX_LEN, KV_WIDTH)
    return pl.pallas_call(
        _attn_lat_kernel,
        out_shape=STREAM_SHAPES[1],
        grid=(N_LAT_SEQ, N_QBLOCKS),
        in_specs=[
            qspec, kvspec, kvspec, cspec, cspec,
            _layer_spec(j, (1, N_HEADS)),
            _layer_spec(j, (D, D)),
            yspec,
            pl.BlockSpec((None, None, N_ADA, D), lambda b, n: (layer, 1 + b, 0, 0)),
        ],
        out_specs=yspec,
        scratch_shapes=[pltpu.VMEM((D, D), BF16)],
        compiler_params=_params(("arbitrary", "arbitrary")),
        name="attn_lat",
    )(q, k, v, ck, cv, sink.reshape(-1, 1, N_HEADS), wo, y_lat, ada)


def _router_kernel(yc_ref, yl_ref, ada_ref, g_ref, rw_ref, rb_ref, sg_ref, su_ref, sd_ref,
                   h_ref, base_ref, key_ref, w_ref, cnt_ref,
                   carry_ref, tri_ref, sgb_ref, sub_ref, sdb_ref):
    i = pl.program_id(0)

    @pl.when(i == 0)
    def _():
        sgb_ref[...] = sg_ref[...].astype(BF16)
        sub_ref[...] = su_ref[...].astype(BF16)
        sdb_ref[...] = sd_ref[...].astype(BF16)
        r = lax.broadcasted_iota(I32, (ROW_TILE, ROW_TILE), 0)
        c = lax.broadcasted_iota(I32, (ROW_TILE, ROW_TILE), 1)
        tri_ref[...] = jnp.where(r < c, 1.0, 0.0).astype(BF16)

    @pl.when(i % TILES_PER_GROUP == 0)
    def _():
        carry_ref[...] = jnp.zeros_like(carry_ref)

    y = _read_stream(yc_ref, yl_ref)
    h = _modulate(y, g_ref[...], ada_ref[3:4, :], ada_ref[4:5, :])
    for c in range(LANE_CHUNKS):
        h_ref[pl.ds(c, ROW_TILE, stride=LANE_CHUNKS), :] = h[:, c * LANES:(c + 1) * LANES]
    hb = h.astype(BF16)

    d = functools.partial(jnp.dot, preferred_element_type=F32)
    act = (_silu(d(hb, sgb_ref[...])) * d(hb, sub_ref[...])).astype(BF16)
    base_ref[...] = y + ada_ref[5:6, :] * d(act, sdb_ref[...])

    logits = _dot3(h, rw_ref[...])
    logits_t = jnp.concatenate([logits, jnp.zeros_like(logits)], axis=1).T[:N_EXPERTS]
    scores = jax.nn.sigmoid(logits_t)
    shape3 = (N_EGROUPS, EGROUP, ROW_TILE)
    scores3 = scores.reshape(shape3)
    choice3 = (scores + rb_ref[...]).reshape(shape3)
    sub = lax.broadcasted_iota(I32, shape3, 1)
    eidx = lax.broadcasted_iota(I32, shape3, 0) * EGROUP + sub
    gidx = lax.broadcasted_iota(I32, (N_EGROUPS, 1, ROW_TILE), 0)
    neg = -jnp.inf

    m1 = jnp.max(choice3, axis=1, keepdims=True)
    i1 = jnp.min(jnp.where(choice3 == m1, sub, EGROUP), axis=1, keepdims=True)
    m2 = jnp.max(jnp.where(sub == i1, neg, choice3), axis=1, keepdims=True)
    gscore = m1 + m2
    allowed = jnp.zeros((N_EGROUPS, 1, ROW_TILE), jnp.bool_)
    for _ in range(TOPK_GROUPS):
        gm = jnp.max(gscore, axis=0, keepdims=True)
        gi = jnp.min(jnp.where(gscore == gm, gidx, N_EGROUPS), axis=0, keepdims=True)
        hit = gidx == gi
        allowed = allowed | hit
        gscore = jnp.where(hit, neg, gscore)

    def reduce_experts(fn, x):
        return fn(fn(x, axis=0, keepdims=True), axis=1, keepdims=True)

    cm = jnp.where(allowed, choice3, neg)
    picked = jnp.zeros(shape3, F32)
    e_rows, s_rows = [], []
    for _ in range(TOP_K):
        m = reduce_experts(jnp.max, cm)
        ik = reduce_experts(jnp.min, jnp.where(cm == m, eidx, N_EXPERTS))
        hit = eidx == ik
        e_rows.append(ik)
        s_rows.append(reduce_experts(jnp.sum, jnp.where(hit, scores3, 0.0)))
        cm = jnp.where(hit, neg, cm)
        picked = jnp.where(hit, 1.0, picked)

    picked2 = picked.reshape(N_EXPERTS, ROW_TILE)
    cum3 = (d(picked2.astype(BF16), tri_ref[...]) + carry_ref[...]).reshape(shape3)
    carry_ref[...] = carry_ref[...] + jnp.sum(picked2, axis=1, keepdims=True)
    cnt_ref[...] = jnp.broadcast_to(carry_ref[...], (N_EXPERTS, LANES))

    denom = functools.reduce(lambda a, b: a + b, s_rows)
    key_rows, w_rows = [], []
    for kk in range(TOP_K):
        rank = reduce_experts(jnp.sum, jnp.where(eidx == e_rows[kk], cum3, 0.0)).astype(I32)
        key_rows.append(((e_rows[kk] << RANK_BITS) | rank).reshape(1, ROW_TILE))
        w_rows.append((s_rows[kk] / denom * ROUTED_SCALE).reshape(1, ROW_TILE))
    key_ref[...] = jnp.concatenate(key_rows, axis=0)
    w_ref[...] = jnp.concatenate(w_rows, axis=0)


def _router_call(y, ada, norm_ffn, router_w, router_bias, sg, su, sd, layer):
    return pl.pallas_call(
        _router_kernel,
        out_shape=(
            jax.ShapeDtypeStruct((N_GROUPS, MOE_GROUP * LANE_CHUNKS, LANES), F32),
            jax.ShapeDtypeStruct((T, D), F32),
            jax.ShapeDtypeStruct((TOP_K, T), I32),
            jax.ShapeDtypeStruct((TOP_K, T), F32),
            jax.ShapeDtypeStruct((N_GROUPS, N_EXPERTS, LANES), F32),
        ),
        grid=(N_TILES,),
        in_specs=STREAM_SPECS + [
            _ada_spec(layer),
            _layer_spec(layer, (1, D)),
            _layer_spec(layer, (D, N_EXPERTS)),
            _layer_spec(layer, (N_EXPERTS, 1)),
            _layer_spec(layer, (D, D_SHARED)),
            _layer_spec(layer, (D, D_SHARED)),
            _layer_spec(layer, (D_SHARED, D)),
        ],
        out_specs=(
            pl.BlockSpec((None, ROW_TILE * LANE_CHUNKS, LANES),
                         lambda i: (i // TILES_PER_GROUP, i % TILES_PER_GROUP, 0)),
            _tile_spec(),
            pl.BlockSpec((TOP_K, ROW_TILE), lambda i: (0, i)),
            pl.BlockSpec((TOP_K, ROW_TILE), lambda i: (0, i)),
            pl.BlockSpec((None, N_EXPERTS, LANES), lambda i: (i // TILES_PER_GROUP, 0, 0)),
        ),
        scratch_shapes=[
            pltpu.VMEM((N_EXPERTS, 1), F32),
            pltpu.VMEM((ROW_TILE, ROW_TILE), BF16),
            pltpu.VMEM((D, D_SHARED), BF16),
            pltpu.VMEM((D, D_SHARED), BF16),
            pltpu.VMEM((D_SHARED, D), BF16),
        ],
        compiler_params=_params(("arbitrary",)),
        name="router",
    )(*y, ada, norm_ffn.reshape(DEPTH, 1, D), router_w, router_bias.reshape(DEPTH, N_EXPERTS, 1),
      sg, su, sd)


SLOT_WORDS = (MOE_NB + 2) * MOE_BM
SLOT_ROWS = SLOT_WORDS // LANES


def _pos_kernel(start_ref, key_ref, pos_ref):
    g = pl.program_id(0)
    key = key_ref[...]
    e = key >> RANK_BITS
    pos = key & ((1 << RANK_BITS) - 1)
    for ee in range(N_EXPERTS):
        pos = pos + jnp.where(e == ee, start_ref[g * N_EXPERTS + ee], 0)
    pos_ref[...] = pos


def _pos_call(starts, keys):
    spec = pl.BlockSpec((TOP_K, MOE_GROUP), lambda g, s: (0, g))
    return pl.pallas_call(
        _pos_kernel,
        out_shape=jax.ShapeDtypeStruct((TOP_K, T), I32),
        grid_spec=pltpu.PrefetchScalarGridSpec(
            num_scalar_prefetch=1, grid=(N_GROUPS,), in_specs=[spec], out_specs=spec),
        compiler_params=_params(("arbitrary",)),
        name="moe_pos",
    )(starts, keys)


SLOT_ROW_BITS = 16
DUMMY_SLOT = MOE_GROUP * SUBLANES
SLOT_UNROLL = 4


def _slot_kernel(pos_ref, init_hbm, slot_ref, sem):
    fill = pltpu.make_async_copy(init_hbm, slot_ref, sem)
    fill.start()
    fill.wait()
    per_token = SUBLANES + (1 << SLOT_ROW_BITS)

    def place(t, carry):
        word = t * per_token
        for k in range(TOP_K):
            slot_ref[pos_ref[k * MOE_GROUP + t]] = word + ((k * MOE_GROUP) << SLOT_ROW_BITS)
        return carry

    lax.fori_loop(0, MOE_GROUP, place, 0, unroll=SLOT_UNROLL)


def _slot_call(pos):
    smem = pl.BlockSpec(memory_space=pltpu.SMEM)
    return pl.pallas_call(
        _slot_kernel,
        out_shape=jax.ShapeDtypeStruct((SLOT_WORDS,), I32),
        in_specs=[smem, pl.BlockSpec(memory_space=pl.ANY)],
        out_specs=smem,
        scratch_shapes=[pltpu.SemaphoreType.DMA],
        name="moe_slots",
    )(pos, jnp.full((SLOT_WORDS,), DUMMY_SLOT, I32))


TILE_ROWS = MOE_GROUP * SUBLANES
ACC_TILE_ROWS = ACC_ROWS * SUBLANES
RMW_ROWS = 16


def _expert_kernel(blk_exp_ref, n_used_ref, next_exp_ref,
                   slot_hbm, w_hbm, x_hbm, wg_hbm, wu_hbm, wd_hbm,
                   out_hbm,
                   x_vmem, acc_vmem, gat0_ref, gat1_ref, yt0_ref, yt1_ref,
                   wgf_ref, wuf_ref, wdf_ref, wgb_ref, wub_ref, wdb_ref,
                   slot_smem, w_smem, turn_ref, sems, wsems, *, layer):
    g = pl.program_id(0)
    j = pl.program_id(1)
    idx = g * MOE_NB + jnp.minimum(j, MOE_NB - 1)
    rows_per_block = MOE_BM // LANES

    def slot_word(block, r):
        return slot_smem[block * rows_per_block + r // LANES, r % LANES]

    def tile_rows(word):
        return pl.ds(pl.multiple_of(word & ((1 << SLOT_ROW_BITS) - 1), SUBLANES), SUBLANES)

    def gather(block, gat_ref):
        for r in range(MOE_BM):
            gat_ref[pl.ds(r * SUBLANES, SUBLANES), :] = x_vmem[tile_rows(slot_word(block, r)), :]

    def expert_mlp(gat_ref, yt_ref):
        xs = jnp.concatenate(
            [gat_ref[pl.ds(c, MOE_BM, stride=LANE_CHUNKS), :] for c in range(LANE_CHUNKS)],
            axis=1).astype(BF16)
        d = functools.partial(jnp.dot, preferred_element_type=F32)
        act = (_silu(d(xs, wgb_ref[...])) * d(xs, wub_ref[...])).astype(BF16)
        yb = d(act, wdb_ref[...])
        for c in range(LANE_CHUNKS):
            yt_ref[pl.ds(c, MOE_BM, stride=LANE_CHUNKS), :] = yb[:, c * LANES:(c + 1) * LANES]

    def combine(block, yt_ref):
        for r0 in range(0, MOE_BM, RMW_ROWS):
            words = [slot_word(block, r) for r in range(r0, r0 + RMW_ROWS)]
            new = [acc_vmem[tile_rows(word), :]
                   + w_smem[word >> SLOT_ROW_BITS] * yt_ref[pl.ds(r * SUBLANES, SUBLANES), :]
                   for r, word in zip(range(r0, r0 + RMW_ROWS), words)]
            for word, tile in zip(words, new):
                acc_vmem[tile_rows(word), :] = tile

    def group_copies():
        return (pltpu.make_async_copy(x_hbm.at[g], x_vmem.at[pl.ds(0, TILE_ROWS)], sems.at[0]),
                pltpu.make_async_copy(slot_hbm.at[g], slot_smem, sems.at[1]),
                pltpu.make_async_copy(w_hbm.at[g], w_smem, sems.at[2]))

    @pl.when(j == 0)
    def _():
        for c in group_copies():
            c.start()
        acc_vmem[...] = jnp.zeros_like(acc_vmem)
        x_vmem[pl.ds(TILE_ROWS, ACC_TILE_ROWS - TILE_ROWS), :] = jnp.zeros(
            (ACC_TILE_ROWS - TILE_ROWS, LANES), F32)
        yt1_ref[...] = jnp.zeros_like(yt1_ref)
        for c in group_copies():
            c.wait()
        gather(0, gat0_ref)

    def weight_copies(e, buf):
        return (pltpu.make_async_copy(wg_hbm.at[layer, e], wgf_ref.at[buf], wsems.at[buf, 0]),
                pltpu.make_async_copy(wu_hbm.at[layer, e], wuf_ref.at[buf], wsems.at[buf, 1]),
                pltpu.make_async_copy(wd_hbm.at[layer, e], wdf_ref.at[buf], wsems.at[buf, 2]))

    e_cur = blk_exp_ref[idx]
    e_prev = blk_exp_ref[jnp.maximum(idx - 1, 0)]

    @pl.when(j == 0)
    def _():
        turn_ref[0] = 0
        for c in weight_copies(e_cur, 0):
            c.start()

    @pl.when((j == 0) | (e_cur != e_prev))
    def _():
        buf = turn_ref[0]
        for c in weight_copies(e_cur, buf):
            c.wait()
        wgb_ref[...] = wgf_ref[buf].astype(BF16)
        wub_ref[...] = wuf_ref[buf].astype(BF16)
        wdb_ref[...] = wdf_ref[buf].astype(BF16)
        e_next = next_exp_ref[g * N_EXPERTS + e_cur]

        @pl.when(e_next >= 0)
        def _():
            for c in weight_copies(e_next, 1 - buf):
                c.start()

        turn_ref[0] = 1 - buf

    active = j <= n_used_ref[g]
    prev_block = jnp.maximum(j - 1, 0)

    @pl.when(active & (j % 2 == 0))
    def _():
        gather(j + 1, gat1_ref)
        expert_mlp(gat0_ref, yt0_ref)
        combine(prev_block, yt1_ref)

    @pl.when(active & (j % 2 == 1))
    def _():
        gather(j + 1, gat0_ref)
        expert_mlp(gat1_ref, yt1_ref)
        combine(prev_block, yt0_ref)

    @pl.when(j == MOE_NB)
    def _():
        out_copy = pltpu.make_async_copy(acc_vmem, out_hbm.at[g], sems.at[3])
        out_copy.start()
        out_copy.wait()


N_WEIGHT_BUFS = 2


def _expert_call(blk_exp, n_used, next_exp, slots, ws, x_tiles, wg, wu, wd, layer):
    any_spec = pl.BlockSpec(memory_space=pl.ANY)
    return pl.pallas_call(
        functools.partial(_expert_kernel, layer=layer),
        out_shape=jax.ShapeDtypeStruct((N_GROUPS, ACC_TILE_ROWS, LANES), F32),
        grid_spec=pltpu.PrefetchScalarGridSpec(
            num_scalar_prefetch=3,
            grid=(N_GROUPS, MOE_NB + 1),
            in_specs=[any_spec] * 6,
            out_specs=any_spec,
            scratch_shapes=[
                pltpu.VMEM((ACC_TILE_ROWS, LANES), F32),
                pltpu.VMEM((ACC_TILE_ROWS, LANES), F32),
                pltpu.VMEM((MOE_BM * SUBLANES, LANES), F32),
                pltpu.VMEM((MOE_BM * SUBLANES, LANES), F32),
                pltpu.VMEM((MOE_BM * SUBLANES, LANES), F32),
                pltpu.VMEM((MOE_BM * SUBLANES, LANES), F32),
                pltpu.VMEM((N_WEIGHT_BUFS, D, D_EXPERT), F32),
                pltpu.VMEM((N_WEIGHT_BUFS, D, D_EXPERT), F32),
                pltpu.VMEM((N_WEIGHT_BUFS, D_EXPERT, D), F32),
                pltpu.VMEM((D, D_EXPERT), BF16),
                pltpu.VMEM((D, D_EXPERT), BF16),
                pltpu.VMEM((D_EXPERT, D), BF16),
                pltpu.SMEM((SLOT_ROWS, LANES), I32),
                pltpu.SMEM((N_ASSIGN,), F32),
                pltpu.SMEM((1,), I32),
                pltpu.SemaphoreType.DMA((4,)),
                pltpu.SemaphoreType.DMA((N_WEIGHT_BUFS, 3)),
            ]),
        compiler_params=_params(("arbitrary", "arbitrary")),
        name="moe_experts",
    )(blk_exp, n_used, next_exp, slots, ws, x_tiles, wg, wu, wd)


def _finish_kernel(base_ref, routed_ref, ada_ref, oc_ref, ol_ref):
    routed = jnp.concatenate(
        [routed_ref[pl.ds(c, ROW_TILE, stride=LANE_CHUNKS), :] for c in range(LANE_CHUNKS)], axis=1)
    _write_stream(oc_ref, ol_ref, base_ref[...] + ada_ref[5:6, :] * routed)


def _finish_call(base, routed, ada, layer):
    rows = ROW_TILE * LANE_CHUNKS
    return pl.pallas_call(
        _finish_kernel,
        out_shape=STREAM_SHAPES,
        grid=(N_TILES,),
        in_specs=[
            _tile_spec(),
            pl.BlockSpec((None, rows, LANES), lambda i: (i // TILES_PER_GROUP, i % TILES_PER_GROUP, 0)),
            _ada_spec(layer),
        ],
        out_specs=STREAM_SPECS,
        compiler_params=_params(("arbitrary",)),
        name="moe_finish",
    )(base, routed, ada)


BM_SHIFT = MOE_BM.bit_length() - 1


def _plan_kernel(cnt_ref, start_ref, n_used_ref, blk_exp_ref, next_exp_ref):
    for g in range(N_GROUPS):
        def expert(e, carry):
            run, last = carry
            blocks = (cnt_ref[g * N_EXPERTS + e] + MOE_BM - 1) >> BM_SHIFT
            start_ref[g * N_EXPERTS + e] = run << BM_SHIFT

            def mark(b, c):
                blk_exp_ref[g * MOE_NB + b] = e
                return c

            lax.fori_loop(run, run + blocks, mark, 0)
            return run + blocks, jnp.where(blocks > 0, e, last)

        used, last = lax.fori_loop(0, N_EXPERTS, expert, (0, 0))
        n_used_ref[g] = used

        def mark_unused(b, c):
            blk_exp_ref[g * MOE_NB + b] = last
            return c

        lax.fori_loop(used, MOE_NB, mark_unused, 0)

        def link(i, nxt):
            e = N_EXPERTS - 1 - i
            next_exp_ref[g * N_EXPERTS + e] = nxt
            return jnp.where(cnt_ref[g * N_EXPERTS + e] > 0, e, nxt)

        lax.fori_loop(0, N_EXPERTS, link, -1)


def _plan_call(counts):
    smem = pl.BlockSpec(memory_space=pltpu.SMEM)
    return pl.pallas_call(
        _plan_kernel,
        out_shape=(jax.ShapeDtypeStruct((N_GROUPS * N_EXPERTS,), I32),
                   jax.ShapeDtypeStruct((N_GROUPS,), I32),
                   jax.ShapeDtypeStruct((N_GROUPS * MOE_NB,), I32),
                   jax.ShapeDtypeStruct((N_GROUPS * N_EXPERTS,), I32)),
        in_specs=[smem],
        out_specs=(smem, smem, smem, smem),
        name="moe_plan",
    )(counts)


def _moe_layer(y, ada, norm_ffn, router_w, router_bias, wg, wu, wd, sg, su, sd, layer):
    x_tiles, base, keys, ws, cnts = _router_call(
        y, ada, norm_ffn, router_w, router_bias, sg, su, sd, layer)
    starts, n_used, blk_exp, next_exp = _plan_call(cnts[:, :, 0].astype(I32).reshape(-1))
    pos = _pos_call(starts, keys)
    slots = jnp.stack([
        _slot_call(pos[:, g * MOE_GROUP:(g + 1) * MOE_GROUP].reshape(-1))
        for g in range(N_GROUPS)]).reshape(N_GROUPS, SLOT_ROWS, LANES)
    ws_g = ws.reshape(TOP_K, N_GROUPS, MOE_GROUP).transpose(1, 0, 2).reshape(N_GROUPS, N_ASSIGN)
    routed = _expert_call(blk_exp, n_used, next_exp, slots, ws_g, x_tiles, wg, wu, wd, layer)
    return _finish_call(base, routed, ada, layer)


def kernel(x_prompt, x_sample, cache_k, cache_v, c, c_ctx, ada_w, ada_b, norm_mix, norm_ffn,
           conv_w1, conv_b1, conv_dw, conv_dw_b, conv_norm, conv_w2, conv_b2,
           attn_wqkv, attn_q_norm, attn_k_norm, attn_sink, attn_wo,
           router_w, router_bias, exp_w_gate, exp_w_up, exp_w_down,
           sh_w_gate, sh_w_up, sh_w_down):
    y = (x_prompt.reshape(TP, D), x_sample.reshape(TS, D))
    cond = jnp.concatenate(
        [c_ctx[None, :], c, jnp.zeros((COND_ROWS - N_COND, D), F32)], axis=0)
    ada = _ada_call(cond, ada_w, ada_b).reshape(DEPTH, COND_ROWS, N_ADA, D)

    new_k = new_v = None
    for layer in range(DEPTH):
        j = layer // 2
        if layer % 2 == 0:
            u = _conv_in_call(y, ada, norm_mix, conv_w1, conv_b1, layer, j)
            y = _conv_out_call(u, y, ada, conv_dw, conv_dw_b, conv_norm, conv_w2, conv_b2, layer, j)
        else:
            q, k, v = _qkv_call(y, ada, norm_mix, attn_wqkv, attn_q_norm, attn_k_norm, layer, j)
            new_k = k[:TP].reshape(N_CTX_SEQ, 1, CTX_LEN, N_KV, HEAD_DIM)
            new_v = v[:TP].reshape(N_CTX_SEQ, 1, CTX_LEN, N_KV, HEAD_DIM)
            y = (_attn_ctx_call(q, k, v, attn_sink, attn_wo, y[0], ada, layer, j),
                 _attn_lat_call(q, k, v, cache_k, cache_v, attn_sink, attn_wo, y[1], ada, layer, j))
        y = _moe_layer(y, ada, norm_ffn, router_w, router_bias, exp_w_gate, exp_w_up, exp_w_down,
                       sh_w_gate, sh_w_up, sh_w_down, layer)

    y_p = y[0].reshape(N_CTX_SEQ, CTX_LEN, D)
    y_s = y[1].reshape(N_LAT_SEQ, LAT_LEN, D)
    return (y_p, y_s, new_k, new_v)
```

```python
import functools

import jax
import jax.numpy as jnp
import numpy as np
from jax import lax
from jax.experimental import pallas as pl
from jax.experimental.pallas import tpu as pltpu

F32 = jnp.float32
BF16 = jnp.bfloat16
I32 = jnp.int32

D = 1024
N_CTX_SEQ = 32
CTX_LEN = 256
N_LAT_SEQ = 4
LAT_LEN = 1024
TP = N_CTX_SEQ * CTX_LEN
TS = N_LAT_SEQ * LAT_LEN
T = TP + TS
DEPTH = 2
N_ADA = 6
N_COND = 1 + N_LAT_SEQ
COND_ROWS = 8
CONV_WIDTH = 31
CONV_PAD = CONV_WIDTH // 2
N_HEADS = 16
N_KV = 4
HEAD_DIM = 64
GQA = N_HEADS // N_KV
KV_WIDTH = N_KV * HEAD_DIM
QKV_WIDTH = D + 2 * KV_WIDTH
WINDOW = 128
GRID_W = 64
ROPE_PAIRS = HEAD_DIM // 4
ROPE_THETA = 10000.0
N_EXPERTS = 64
N_EGROUPS = 8
EGROUP = N_EXPERTS // N_EGROUPS
TOPK_GROUPS = 4
TOP_K = 8
D_EXPERT = 256
D_SHARED = 256
ROUTED_SCALE = 2.5
NORM_EPS = 1e-6

LANES = 128
SUBLANES = 8
VMEM_LIMIT = 56 * 1024 * 1024

ROW_TILE = 256
N_TILES = T // ROW_TILE
N_CTX_TILES = TP // ROW_TILE
LAT_TILES_PER_SEQ = LAT_LEN // ROW_TILE
HALO = 16
LANE_CHUNKS = D // LANES

MOE_GROUP = 4096
N_GROUPS = T // MOE_GROUP
TILES_PER_GROUP = MOE_GROUP // ROW_TILE
MOE_BM = 256
MOE_NB = MOE_GROUP * TOP_K // MOE_BM + N_EXPERTS
N_ASSIGN = MOE_GROUP * TOP_K
ACC_ROWS = MOE_GROUP + SUBLANES
RANK_BITS = 12
Q_BLOCK = 128


def _cond_of_tile(i):
    return jnp.where(i < N_CTX_TILES, 0, 1 + (i - N_CTX_TILES) // LAT_TILES_PER_SEQ)


def _params(sem, vmem=VMEM_LIMIT):
    return pltpu.CompilerParams(dimension_semantics=sem, vmem_limit_bytes=vmem)


def _bdot(a, b):
    return jnp.dot(a.astype(BF16), b.astype(BF16), preferred_element_type=F32)


def _split(a):
    hi = a.astype(BF16)
    lo = (a - hi.astype(F32)).astype(BF16)
    return hi, lo


def _dot3(a, b):
    a_hi, a_lo = _split(a)
    b_hi, b_lo = _split(b)
    d = functools.partial(jnp.dot, preferred_element_type=F32)
    return d(a_hi, b_hi) + d(a_lo, b_hi) + d(a_hi, b_lo)


def _dot2(a, b_bf16):
    a_hi, a_lo = _split(a)
    d = functools.partial(jnp.dot, preferred_element_type=F32)
    return d(a_hi, b_bf16) + d(a_lo, b_bf16)


def _rms(x, g):
    return x * lax.rsqrt(jnp.mean(x * x, axis=-1, keepdims=True) + NORM_EPS) * g


def _modulate(x, g, shift, scale):
    return _rms(x, g) * (1.0 + scale) + shift


def _silu(x):
    return x * jax.nn.sigmoid(x)


def _ada_spec(layer):
    return pl.BlockSpec((None, None, N_ADA, D), lambda i: (layer, _cond_of_tile(i), 0, 0))


def _tile_spec(width=D):
    return pl.BlockSpec((ROW_TILE, width), lambda i: (i, 0))


STREAM_SPECS = [
    pl.BlockSpec((ROW_TILE, D), lambda i: (jnp.minimum(i, N_CTX_TILES - 1), 0)),
    pl.BlockSpec((ROW_TILE, D), lambda i: (jnp.maximum(i - N_CTX_TILES, 0), 0)),
]
STREAM_SHAPES = (jax.ShapeDtypeStruct((TP, D), F32), jax.ShapeDtypeStruct((TS, D), F32))


def _read_stream(ctx_ref, lat_ref):
    return jnp.where(pl.program_id(0) < N_CTX_TILES, ctx_ref[...], lat_ref[...])


def _write_stream(ctx_ref, lat_ref, value):
    i = pl.program_id(0)

    @pl.when(i < N_CTX_TILES)
    def _():
        ctx_ref[...] = value

    @pl.when(i >= N_CTX_TILES)
    def _():
        lat_ref[...] = value


def _const_spec(shape):
    nd = len(shape)
    return pl.BlockSpec(shape, lambda *_: (0,) * nd)


def _layer_spec(layer, shape):
    nd = len(shape)
    return pl.BlockSpec((None,) + tuple(shape), lambda *_: (layer,) + (0,) * nd)


ADA_NB = 512


def _ada_kernel(c_ref, w_ref, b_ref, o_ref):
    o_ref[...] = _dot3(_silu(c_ref[...]), w_ref[...]) + b_ref[...]


def _ada_call(cond, ada_w, ada_b):
    return pl.pallas_call(
        _ada_kernel,
        out_shape=jax.ShapeDtypeStruct((DEPTH, COND_ROWS, N_ADA * D), F32),
        grid=(DEPTH, N_ADA * D // ADA_NB),
        in_specs=[
            pl.BlockSpec((COND_ROWS, D), lambda l, n: (0, 0)),
            pl.BlockSpec((None, D, ADA_NB), lambda l, n: (l, 0, n)),
            pl.BlockSpec((None, 1, ADA_NB), lambda l, n: (l, 0, n)),
        ],
        out_specs=pl.BlockSpec((None, COND_ROWS, ADA_NB), lambda l, n: (l, 0, n)),
        compiler_params=_params(("arbitrary", "arbitrary")),
        name="ada_params",
    )(cond, ada_w, ada_b.reshape(DEPTH, 1, N_ADA * D))


def _conv_in_kernel(xc_ref, xl_ref, ada_ref, g_ref, w1_ref, b1_ref, u_ref, w1b_ref):
    @pl.when(pl.program_id(0) == 0)
    def _():
        w1b_ref[...] = w1_ref[...].astype(BF16)

    h = _modulate(_read_stream(xc_ref, xl_ref), g_ref[...], ada_ref[0:1, :], ada_ref[1:2, :])
    u = jnp.dot(h.astype(BF16), w1b_ref[...], preferred_element_type=F32) + b1_ref[...]
    u_ref[...] = u[:, :D] * jax.nn.sigmoid(u[:, D:])


def _conv_in_call(y, ada, norm_mix, conv_w1, conv_b1, layer, j):
    return pl.pallas_call(
        _conv_in_kernel,
        out_shape=jax.ShapeDtypeStruct((T, D), F32),
        grid=(N_TILES,),
        in_specs=STREAM_SPECS + [
            _ada_spec(layer),
            _layer_spec(layer, (1, D)),
            _layer_spec(j, (D, 2 * D)),
            _layer_spec(j, (1, 2 * D)),
        ],
        out_specs=_tile_spec(),
        scratch_shapes=[pltpu.VMEM((D, 2 * D), BF16)],
        compiler_params=_params(("arbitrary",)),
        name="conv_in",
    )(*y, ada, norm_mix.reshape(DEPTH, 1, D), conv_w1, conv_b1.reshape(-1, 1, 2 * D))


CONV_ROWS = 64
BUF_ROWS = ROW_TILE + 2 * HALO


def _conv_out_kernel(u_ref, up_ref, un_ref, yc_ref, yl_ref, ada_ref, dw_ref, dwb_ref, gn_ref,
                     w2_ref, b2_ref, oc_ref, ol_ref, buf_ref, z_ref, w2b_ref):
    i = pl.program_id(0)

    @pl.when(i == 0)
    def _():
        w2b_ref[...] = w2_ref[...].astype(BF16)

    lat = i >= N_CTX_TILES
    pos = jnp.where(lat, i - N_CTX_TILES, 0) % LAT_TILES_PER_SEQ
    has_prev = lat & (pos != 0)
    has_next = lat & (pos != LAT_TILES_PER_SEQ - 1)
    for c in range(LANE_CHUNKS):
        cs = slice(c * LANES, (c + 1) * LANES)
        buf_ref[c, 0:HALO, :] = jnp.where(has_prev, up_ref[:, cs], 0.0)
        buf_ref[c, HALO:HALO + ROW_TILE, :] = u_ref[:, cs]
        buf_ref[c, HALO + ROW_TILE:BUF_ROWS, :] = jnp.where(has_next, un_ref[:, cs], 0.0)

    off = HALO - CONV_PAD
    for c in range(LANE_CHUNKS):
        cs = slice(c * LANES, (c + 1) * LANES)
        for r0 in range(0, ROW_TILE, CONV_ROWS):
            acc = jnp.broadcast_to(dwb_ref[:, cs], (CONV_ROWS, LANES))
            for k in range(CONV_WIDTH):
                win = buf_ref[c, r0 + k + off:r0 + k + off + CONV_ROWS, :]
                acc = acc + dw_ref[k:k + 1, cs] * win
            z_ref[r0:r0 + CONV_ROWS, cs] = acc

    z = _silu(_rms(z_ref[...], gn_ref[...]))
    m = jnp.dot(z.astype(BF16), w2b_ref[...], preferred_element_type=F32) + b2_ref[...]
    _write_stream(oc_ref, ol_ref, _read_stream(yc_ref, yl_ref) + ada_ref[2:3, :] * m)


def _conv_out_call(u, y, ada, conv_dw, conv_dw_b, conv_norm, conv_w2, conv_b2, layer, j):
    halos_per_tile = ROW_TILE // HALO
    last_halo = T // HALO - 1
    return pl.pallas_call(
        _conv_out_kernel,
        out_shape=STREAM_SHAPES,
        grid=(N_TILES,),
        in_specs=[
            _tile_spec(),
            pl.BlockSpec((HALO, D), lambda i: (jnp.maximum(i * halos_per_tile - 1, 0), 0)),
            pl.BlockSpec((HALO, D), lambda i: (jnp.minimum((i + 1) * halos_per_tile, last_halo), 0)),
        ] + STREAM_SPECS + [
            _ada_spec(layer),
            _layer_spec(j, (CONV_WIDTH + 1, D)),
            _layer_spec(j, (1, D)),
            _layer_spec(j, (1, D)),
            _layer_spec(j, (D, D)),
            _layer_spec(j, (1, D)),
        ],
        out_specs=STREAM_SPECS,
        scratch_shapes=[pltpu.VMEM((LANE_CHUNKS, BUF_ROWS, LANES), F32), pltpu.VMEM((ROW_TILE, D), F32),
                        pltpu.VMEM((D, D), BF16)],
        compiler_params=_params(("arbitrary",)),
        name="conv_out",
    )(u, u, u, *y, ada, jnp.pad(conv_dw, ((0, 0), (0, 1), (0, 0))), conv_dw_b.reshape(-1, 1, D),
      conv_norm.reshape(-1, 1, D), conv_w2, conv_b2.reshape(-1, 1, D))


def _swap_pairs(x, width):
    lane = lax.broadcasted_iota(I32, x.shape, 1)
    first = (lane % (2 * ROPE_PAIRS)) < ROPE_PAIRS
    return jnp.where(first, pltpu.roll(x, width - ROPE_PAIRS, 1), pltpu.roll(x, ROPE_PAIRS, 1))


def _qkv_kernel(xc_ref, xl_ref, ada_ref, g_ref, w_ref, qn_ref, kn_ref, cos_ref, sin_ref,
                q_ref, k_ref, v_ref, wb_ref, hs_ref):
    i = pl.program_id(0)

    @pl.when(i == 0)
    def _():
        wb_ref[...] = w_ref[...].astype(BF16)
        r = lax.broadcasted_iota(I32, (D, D), 0) // HEAD_DIM
        c = lax.broadcasted_iota(I32, (D, D), 1) // HEAD_DIM
        hs_ref[...] = jnp.where(r == c, 1.0 / HEAD_DIM, 0.0).astype(BF16)

    h = _modulate(_read_stream(xc_ref, xl_ref), g_ref[...], ada_ref[0:1, :], ada_ref[1:2, :])
    qkv = jnp.dot(h.astype(BF16), wb_ref[...], preferred_element_type=F32)
    q = qkv[:, :D]
    k = qkv[:, D:D + KV_WIDTH]
    v_ref[...] = qkv[:, D + KV_WIDTH:]
    q_ms = _dot2(q * q, hs_ref[...])
    k_ms = _dot2(k * k, hs_ref[0:KV_WIDTH, 0:KV_WIDTH])
    qn = q * lax.rsqrt(q_ms + NORM_EPS) * qn_ref[...]
    kn = k * lax.rsqrt(k_ms + NORM_EPS) * kn_ref[...]

    @pl.when(i < N_CTX_TILES)
    def _():
        q_ref[...] = qn
        k_ref[...] = kn

    @pl.when(i >= N_CTX_TILES)
    def _():
        cos = cos_ref[...]
        sin = sin_ref[...]
        q_ref[...] = qn * cos + _swap_pairs(qn, D) * sin
        k_ref[...] = kn * cos[:, :KV_WIDTH] + _swap_pairs(kn, KV_WIDTH) * sin[:, :KV_WIDTH]


def _rope_tables():
    pos = np.arange(LAT_LEN)
    row = (pos // GRID_W).astype(np.float32)
    col = (pos % GRID_W).astype(np.float32)
    inv_freq = np.float32(ROPE_THETA) ** (-np.arange(ROPE_PAIRS, dtype=np.float32) / ROPE_PAIRS)
    d = np.arange(D) % HEAD_DIM
    freq = inv_freq[d % ROPE_PAIRS].astype(np.float32)
    p = np.where((d >= HEAD_DIM // 2)[None, :], col[:, None], row[:, None])
    ang = (p * freq[None, :]).astype(np.float32)
    first = (d % (2 * ROPE_PAIRS)) < ROPE_PAIRS
    cos = np.cos(ang).astype(np.float32)
    sin = np.sin(ang).astype(np.float32)
    return jnp.asarray(cos), jnp.asarray(np.where(first[None, :], -sin, sin))


def _qkv_call(y, ada, norm_mix, wqkv, q_norm, k_norm, layer, j):
    cos, sin = _rope_tables()
    table_spec = pl.BlockSpec(
        (ROW_TILE, D), lambda i: (jnp.maximum(i - N_CTX_TILES, 0) % LAT_TILES_PER_SEQ, 0))
    return pl.pallas_call(
        _qkv_kernel,
        out_shape=(jax.ShapeDtypeStruct((T, D), F32), jax.ShapeDtypeStruct((T, KV_WIDTH), F32),
                   jax.ShapeDtypeStruct((T, KV_WIDTH), F32)),
        grid=(N_TILES,),
        in_specs=STREAM_SPECS + [
            _ada_spec(layer),
            _layer_spec(layer, (1, D)),
            _layer_spec(j, (D, QKV_WIDTH)),
            _const_spec((1, D)),
            _const_spec((1, KV_WIDTH)),
            table_spec,
            table_spec,
        ],
        out_specs=(_tile_spec(), _tile_spec(KV_WIDTH), _tile_spec(KV_WIDTH)),
        scratch_shapes=[pltpu.VMEM((D, QKV_WIDTH), BF16), pltpu.VMEM((D, D), BF16)],
        compiler_params=_params(("arbitrary",)),
        name="qkv",
    )(*y, ada, norm_mix.reshape(DEPTH, 1, D), wqkv,
      jnp.tile(q_norm[j], N_HEADS).reshape(1, D), jnp.tile(k_norm[j], N_KV).reshape(1, KV_WIDTH),
      cos, sin)


HEADS_PER_TILE = LANES // HEAD_DIM
TILES_PER_KV = GQA // HEADS_PER_TILE


def _gqa_attention(q, segments, sink_ref, rows):
    d = functools.partial(jnp.dot, preferred_element_type=F32)
    half = lax.broadcasted_iota(I32, (rows, LANES), 1) < HEAD_DIM
    q_zero = jnp.zeros((rows, LANES), BF16)

    scores, sinks, kvs = [], [], []
    for kv in range(N_KV):
        sl = slice(kv * HEAD_DIM, (kv + 1) * HEAD_DIM)
        parts = []
        for t in range(TILES_PER_KV):
            tile = kv * TILES_PER_KV + t
            qt = q[:, tile * LANES:(tile + 1) * LANES]
            parts += [jnp.where(half, qt, q_zero), jnp.where(half, q_zero, qt)]
        qg = jnp.concatenate(parts, axis=0)
        seg_scores, seg_kv = [], []
        for k, v, bias in segments:
            kk = jnp.concatenate([k[:, sl], k[:, sl]], axis=1)
            s = lax.dot_general(qg, kk, (((1,), (1,)), ((), ())),
                                preferred_element_type=F32) * (HEAD_DIM ** -0.5)
            seg_scores.append(s if bias is None else s + bias)
            v_zero = jnp.zeros_like(v[:, sl])
            seg_kv.append((jnp.concatenate([v[:, sl], v_zero], axis=1),
                           jnp.concatenate([v_zero, v[:, sl]], axis=1)))
        scores.append(seg_scores)
        kvs.append(seg_kv)
        sinks.append(jnp.concatenate(
            [jnp.broadcast_to(sink_ref[0:1, kv * GQA + g:kv * GQA + g + 1], (rows, 1))
             for g in range(GQA)], axis=0))

    maxes = []
    for kv in range(N_KV):
        m = sinks[kv]
        for s in scores[kv]:
            m = jnp.maximum(m, jnp.max(s, axis=1, keepdims=True))
        maxes.append(m)
    probs = [[jnp.exp(s - maxes[kv]).astype(BF16) for s in scores[kv]] for kv in range(N_KV)]

    tiles = []
    for kv in range(N_KV):
        den = jnp.exp(sinks[kv] - maxes[kv])
        for p in probs[kv]:
            den = den + d(p, jnp.ones((p.shape[1], LANES), BF16))
        inv = 1.0 / den
        for t in range(TILES_PER_KV):
            tile = None
            for h in range(HEADS_PER_TILE):
                g = t * HEADS_PER_TILE + h
                rs = slice(g * rows, (g + 1) * rows)
                o = None
                for p, v_halves in zip(probs[kv], kvs[kv]):
                    part = d(p[rs, :], v_halves[h])
                    o = part if o is None else o + part
                o = o * inv[rs, :]
                tile = o if tile is None else tile + o
            tiles.append(tile)
    return jnp.concatenate(tiles, axis=1)


def _attn_ctx_kernel(q_ref, k_ref, v_ref, sink_ref, wo_ref, y_ref, ada_ref, o_ref, wob_ref):
    @pl.when(pl.program_id(0) == 0)
    def _():
        wob_ref[...] = wo_ref[...].astype(BF16)

    segments = [(k_ref[...].astype(BF16), v_ref[...].astype(BF16), None)]
    o = _gqa_attention(q_ref[...].astype(BF16), segments, sink_ref, CTX_LEN)
    m_out = jnp.dot(o.astype(BF16), wob_ref[...], preferred_element_type=F32)
    o_ref[...] = y_ref[...] + ada_ref[2:3, :] * m_out


def _attn_ctx_call(q, k, v, sink, wo, y_ctx, ada, layer, j):
    return pl.pallas_call(
        _attn_ctx_kernel,
        out_shape=STREAM_SHAPES[0],
        grid=(N_CTX_SEQ,),
        in_specs=[
            _tile_spec(), _tile_spec(KV_WIDTH), _tile_spec(KV_WIDTH),
            _layer_spec(j, (1, N_HEADS)),
            _layer_spec(j, (D, D)),
            _tile_spec(),
            _ada_spec(layer),
        ],
        out_specs=_tile_spec(),
        scratch_shapes=[pltpu.VMEM((D, D), BF16)],
        compiler_params=_params(("arbitrary",)),
        name="attn_ctx",
    )(q, k, v, sink.reshape(-1, 1, N_HEADS), wo, y_ctx, ada)


LAT_WIN = 3 * WINDOW
N_QBLOCKS = LAT_LEN // Q_BLOCK


def _attn_lat_kernel(q_ref, k_ref, v_ref, kc_ref, vc_ref, sink_ref, wo_ref, y_ref, ada_ref,
                     o_ref, wob_ref):
    b = pl.program_id(0)
    n = pl.program_id(1)

    @pl.when((b == 0) & (n == 0))
    def _():
        wob_ref[...] = wo_ref[...].astype(BF16)

    start = pl.multiple_of(jnp.clip((n - 1) * WINDOW, 0, LAT_LEN - LAT_WIN), WINDOW)
    kw = k_ref[pl.ds(start, LAT_WIN), :].astype(BF16)
    vw = v_ref[pl.ds(start, LAT_WIN), :].astype(BF16)
    qpos = n * Q_BLOCK + lax.broadcasted_iota(I32, (GQA * Q_BLOCK, LAT_WIN), 0) % Q_BLOCK
    kpos = start + lax.broadcasted_iota(I32, (GQA * Q_BLOCK, LAT_WIN), 1)
    band = jnp.where(jnp.abs(kpos - qpos) <= WINDOW, 0.0, -jnp.inf)
    segments = [(kw, vw, band),
                (kc_ref[...].astype(BF16), vc_ref[...].astype(BF16), None)]
    o = _gqa_attention(q_ref[...].astype(BF16), segments, sink_ref, Q_BLOCK)
    m_out = jnp.dot(o.astype(BF16), wob_ref[...], preferred_element_type=F32)
    o_ref[...] = y_ref[...] + ada_ref[2:3, :] * m_out


def _attn_lat_call(q, k, v, cache_k, cache_v, sink, wo, y_lat, ada, layer, j):
    q_row0 = TP // Q_BLOCK
    seq0 = TP // LAT_LEN
    qspec = pl.BlockSpec((Q_BLOCK, D), lambda b, n: (q_row0 + b * N_QBLOCKS + n, 0))
    yspec = pl.BlockSpec((Q_BLOCK, D), lambda b, n: (b * N_QBLOCKS + n, 0))
    kvspec = pl.BlockSpec((LAT_LEN, KV_WIDTH), lambda b, n: (seq0 + b, 0))
    cspec = pl.BlockSpec((None, None, CTX_LEN, KV_WIDTH), lambda b, n: (b, j, 0, 0))
    ck = cache_k.reshape(N_LAT_SEQ, -1, CTX_LEN, KV_WIDTH)
    cv = cache_v.reshape(N_LAT_SEQ, -1, CTX_LEN, KV_WIDTH)
    return pl.pallas_call(
        _attn_lat_kernel,
        out_shape=STREAM_SHAPES[1],
        grid=(N_LAT_SEQ, N_QBLOCKS),
        in_specs=[
            qspec, kvspec, kvspec, cspec, cspec,
            _layer_spec(j, (1, N_HEADS)),
            _layer_spec(j, (D, D)),
            yspec,
            pl.BlockSpec((None, None, N_ADA, D), lambda b, n: (layer, 1 + b, 0, 0)),
        ],
        out_specs=yspec,
        scratch_shapes=[pltpu.VMEM((D, D), BF16)],
        compiler_params=_params(("arbitrary", "arbitrary")),
        name="attn_lat",
    )(q, k, v, ck, cv, sink.reshape(-1, 1, N_HEADS), wo, y_lat, ada)


def _router_kernel(yc_ref, yl_ref, ada_ref, g_ref, rw_ref, rb_ref, sg_ref, su_ref, sd_ref,
                   h_ref, base_ref, key_ref, w_ref, cnt_ref,
                   carry_ref, tri_ref, sgb_ref, sub_ref, sdb_ref):
    i = pl.program_id(0)

    @pl.when(i == 0)
    def _():
        sgb_ref[...] = sg_ref[...].astype(BF16)
        sub_ref[...] = su_ref[...].astype(BF16)
        sdb_ref[...] = sd_ref[...].astype(BF16)
        r = lax.broadcasted_iota(I32, (ROW_TILE, ROW_TILE), 0)
        c = lax.broadcasted_iota(I32, (ROW_TILE, ROW_TILE), 1)
        tri_ref[...] = jnp.where(r < c, 1.0, 0.0).astype(BF16)

    @pl.when(i % TILES_PER_GROUP == 0)
    def _():
        carry_ref[...] = jnp.zeros_like(carry_ref)

    y = _read_stream(yc_ref, yl_ref)
    h = _modulate(y, g_ref[...], ada_ref[3:4, :], ada_ref[4:5, :])
    for c in range(LANE_CHUNKS):
        h_ref[pl.ds(c, ROW_TILE, stride=LANE_CHUNKS), :] = h[:, c * LANES:(c + 1) * LANES]
    hb = h.astype(BF16)

    d = functools.partial(jnp.dot, preferred_element_type=F32)
    act = (_silu(d(hb, sgb_ref[...])) * d(hb, sub_ref[...])).astype(BF16)
    base_ref[...] = y + ada_ref[5:6, :] * d(act, sdb_ref[...])

    logits = _dot3(h, rw_ref[...])
    logits_t = jnp.concatenate([logits, jnp.zeros_like(logits)], axis=1).T[:N_EXPERTS]
    scores = jax.nn.sigmoid(logits_t)
    shape3 = (N_EGROUPS, EGROUP, ROW_TILE)
    scores3 = scores.reshape(shape3)
    choice3 = (scores + rb_ref[...]).reshape(shape3)
    sub = lax.broadcasted_iota(I32, shape3, 1)
    eidx = lax.broadcasted_iota(I32, shape3, 0) * EGROUP + sub
    gidx = lax.broadcasted_iota(I32, (N_EGROUPS, 1, ROW_TILE), 0)
    neg = -jnp.inf

    m1 = jnp.max(choice3, axis=1, keepdims=True)
    i1 = jnp.min(jnp.where(choice3 == m1, sub, EGROUP), axis=1, keepdims=True)
    m2 = jnp.max(jnp.where(sub == i1, neg, choice3), axis=1, keepdims=True)
    gscore = m1 + m2
    allowed = jnp.zeros((N_EGROUPS, 1, ROW_TILE), jnp.bool_)
    for _ in range(TOPK_GROUPS):
        gm = jnp.max(gscore, axis=0, keepdims=True)
        gi = jnp.min(jnp.where(gscore == gm, gidx, N_EGROUPS), axis=0, keepdims=True)
        hit = gidx == gi
        allowed = allowed | hit
        gscore = jnp.where(hit, neg, gscore)

    def reduce_experts(fn, x):
        return fn(fn(x, axis=0, keepdims=True), axis=1, keepdims=True)

    cm = jnp.where(allowed, choice3, neg)
    picked = jnp.zeros(shape3, F32)
    e_rows, s_rows = [], []
    for _ in range(TOP_K):
        m = reduce_experts(jnp.max, cm)
        ik = reduce_experts(jnp.min, jnp.where(cm == m, eidx, N_EXPERTS))
        hit = eidx == ik
        e_rows.append(ik)
        s_rows.append(reduce_experts(jnp.sum, jnp.where(hit, scores3, 0.0)))
        cm = jnp.where(hit, neg, cm)
        picked = jnp.where(hit, 1.0, picked)

    picked2 = picked.reshape(N_EXPERTS, ROW_TILE)
    cum3 = (d(picked2.astype(BF16), tri_ref[...]) + carry_ref[...]).reshape(shape3)
    carry_ref[...] = carry_ref[...] + jnp.sum(picked2, axis=1, keepdims=True)
    cnt_ref[...] = jnp.broadcast_to(carry_ref[...], (N_EXPERTS, LANES))

    denom = functools.reduce(lambda a, b: a + b, s_rows)
    key_rows, w_rows = [], []
    for kk in range(TOP_K):
        rank = reduce_experts(jnp.sum, jnp.where(eidx == e_rows[kk], cum3, 0.0)).astype(I32)
        key_rows.append(((e_rows[kk] << RANK_BITS) | rank).reshape(1, ROW_TILE))
        w_rows.append((s_rows[kk] / denom * ROUTED_SCALE).reshape(1, ROW_TILE))
    key_ref[...] = jnp.concatenate(key_rows, axis=0)
    w_ref[...] = jnp.concatenate(w_rows, axis=0)


def _router_call(y, ada, norm_ffn, router_w, router_bias, sg, su, sd, layer):
    return pl.pallas_call(
        _router_kernel,
        out_shape=(
            jax.ShapeDtypeStruct((N_GROUPS, MOE_GROUP * LANE_CHUNKS, LANES), F32),
            jax.ShapeDtypeStruct((T, D), F32),
            jax.ShapeDtypeStruct((TOP_K, T), I32),
            jax.ShapeDtypeStruct((TOP_K, T), F32),
            jax.ShapeDtypeStruct((N_GROUPS, N_EXPERTS, LANES), F32),
        ),
        grid=(N_TILES,),
        in_specs=STREAM_SPECS + [
            _ada_spec(layer),
            _layer_spec(layer, (1, D)),
            _layer_spec(layer, (D, N_EXPERTS)),
            _layer_spec(layer, (N_EXPERTS, 1)),
            _layer_spec(layer, (D, D_SHARED)),
            _layer_spec(layer, (D, D_SHARED)),
            _layer_spec(layer, (D_SHARED, D)),
        ],
        out_specs=(
            pl.BlockSpec((None, ROW_TILE * LANE_CHUNKS, LANES),
                         lambda i: (i // TILES_PER_GROUP, i % TILES_PER_GROUP, 0)),
            _tile_spec(),
            pl.BlockSpec((TOP_K, ROW_TILE), lambda i: (0, i)),
            pl.BlockSpec((TOP_K, ROW_TILE), lambda i: (0, i)),
            pl.BlockSpec((None, N_EXPERTS, LANES), lambda i: (i // TILES_PER_GROUP, 0, 0)),
        ),
        scratch_shapes=[
            pltpu.VMEM((N_EXPERTS, 1), F32),
            pltpu.VMEM((ROW_TILE, ROW_TILE), BF16),
            pltpu.VMEM((D, D_SHARED), BF16),
            pltpu.VMEM((D, D_SHARED), BF16),
            pltpu.VMEM((D_SHARED, D), BF16),
        ],
        compiler_params=_params(("arbitrary",)),
        name="router",
    )(*y, ada, norm_ffn.reshape(DEPTH, 1, D), router_w, router_bias.reshape(DEPTH, N_EXPERTS, 1),
      sg, su, sd)


SLOT_WORDS = (MOE_NB + 3) * MOE_BM
SLOT_ROWS = SLOT_WORDS // LANES


def _pos_kernel(start_ref, key_ref, pos_ref):
    g = pl.program_id(0)
    key = key_ref[...]
    e = key >> RANK_BITS
    pos = key & ((1 << RANK_BITS) - 1)
    for ee in range(N_EXPERTS):
        pos = pos + jnp.where(e == ee, start_ref[g * N_EXPERTS + ee], 0)
    pos_ref[...] = pos


def _pos_call(starts, keys):
    spec = pl.BlockSpec((TOP_K, MOE_GROUP), lambda g, s: (0, g))
    return pl.pallas_call(
        _pos_kernel,
        out_shape=jax.ShapeDtypeStruct((TOP_K, T), I32),
        grid_spec=pltpu.PrefetchScalarGridSpec(
            num_scalar_prefetch=1, grid=(N_GROUPS,), in_specs=[spec], out_specs=spec),
        compiler_params=_params(("arbitrary",)),
        name="moe_pos",
    )(starts, keys)


SLOT_ROW_BITS = 16
DUMMY_SLOT = MOE_GROUP * SUBLANES
SLOT_UNROLL = 4


def _slot_kernel(pos_ref, init_hbm, slot_ref, sem):
    fill = pltpu.make_async_copy(init_hbm, slot_ref, sem)
    fill.start()
    fill.wait()
    per_token = SUBLANES + (1 << SLOT_ROW_BITS)

    def place(t, carry):
        word = t * per_token
        for k in range(TOP_K):
            slot_ref[pos_ref[k * MOE_GROUP + t]] = word + ((k * MOE_GROUP) << SLOT_ROW_BITS)
        return carry

    lax.fori_loop(0, MOE_GROUP, place, 0, unroll=SLOT_UNROLL)


def _slot_call(pos):
    smem = pl.BlockSpec(memory_space=pltpu.SMEM)
    return pl.pallas_call(
        _slot_kernel,
        out_shape=jax.ShapeDtypeStruct((SLOT_WORDS,), I32),
        in_specs=[smem, pl.BlockSpec(memory_space=pl.ANY)],
        out_specs=smem,
        scratch_shapes=[pltpu.SemaphoreType.DMA],
        name="moe_slots",
    )(pos, jnp.full((SLOT_WORDS,), DUMMY_SLOT, I32))


TILE_ROWS = MOE_GROUP * SUBLANES
ACC_TILE_ROWS = ACC_ROWS * SUBLANES
RMW_ROWS = 16


def _expert_kernel(blk_exp_ref, n_used_ref, next_exp_ref,
                   slot_hbm, w_hbm, x_hbm, wg_hbm, wu_hbm, wd_hbm,
                   out_hbm,
                   x_vmem, acc_vmem, gat0_ref, gat1_ref, yt0_ref, yt1_ref,
                   wgf_ref, wuf_ref, wdf_ref, wgb_ref, wub_ref, wdb_ref,
                   slot_smem, w_smem, turn_ref, sems, wsems, *, layer):
    g = pl.program_id(0)
    rows_per_block = MOE_BM // LANES

    def slot_word(block, r):
        return slot_smem[block * rows_per_block + r // LANES, r % LANES]

    def tile_rows(word):
        return pl.ds(pl.multiple_of(word & ((1 << SLOT_ROW_BITS) - 1), SUBLANES), SUBLANES)

    def gather(block, gat_ref):
        for r in range(MOE_BM):
            gat_ref[pl.ds(r * SUBLANES, SUBLANES), :] = x_vmem[tile_rows(slot_word(block, r)), :]

    def expert_mlp(gat_ref, yt_ref):
        xs = jnp.concatenate(
            [gat_ref[pl.ds(c, MOE_BM, stride=LANE_CHUNKS), :] for c in range(LANE_CHUNKS)],
            axis=1).astype(BF16)
        d = functools.partial(jnp.dot, preferred_element_type=F32)
        act = (_silu(d(xs, wgb_ref[...])) * d(xs, wub_ref[...])).astype(BF16)
        yb = d(act, wdb_ref[...])
        for c in range(LANE_CHUNKS):
            yt_ref[pl.ds(c, MOE_BM, stride=LANE_CHUNKS), :] = yb[:, c * LANES:(c + 1) * LANES]

    def combine(block, yt_ref):
        for r0 in range(0, MOE_BM, RMW_ROWS):
            words = [slot_word(block, r) for r in range(r0, r0 + RMW_ROWS)]
            new = [acc_vmem[tile_rows(word), :]
                   + w_smem[word >> SLOT_ROW_BITS] * yt_ref[pl.ds(r * SUBLANES, SUBLANES), :]
                   for r, word in zip(range(r0, r0 + RMW_ROWS), words)]
            for word, tile in zip(words, new):
                acc_vmem[tile_rows(word), :] = tile

    def group_copies():
        return (pltpu.make_async_copy(x_hbm.at[g], x_vmem.at[pl.ds(0, TILE_ROWS)], sems.at[0]),
                pltpu.make_async_copy(slot_hbm.at[g], slot_smem, sems.at[1]),
                pltpu.make_async_copy(w_hbm.at[g], w_smem, sems.at[2]))

    def weight_copies(e, buf):
        return (pltpu.make_async_copy(wg_hbm.at[layer, e], wgf_ref.at[buf], wsems.at[buf, 0]),
                pltpu.make_async_copy(wu_hbm.at[layer, e], wuf_ref.at[buf], wsems.at[buf, 1]),
                pltpu.make_async_copy(wd_hbm.at[layer, e], wdf_ref.at[buf], wsems.at[buf, 2]))

    for c in group_copies():
        c.start()
    turn_ref[0] = 0
    for c in weight_copies(blk_exp_ref[g * MOE_NB], 0):
        c.start()
    acc_vmem[...] = jnp.zeros_like(acc_vmem)
    x_vmem[pl.ds(TILE_ROWS, ACC_TILE_ROWS - TILE_ROWS), :] = jnp.zeros(
        (ACC_TILE_ROWS - TILE_ROWS, LANES), F32)
    yt1_ref[...] = jnp.zeros_like(yt1_ref)
    for c in group_copies():
        c.wait()
    gather(0, gat0_ref)

    def step(j, gat_cur, gat_next, yt_cur, yt_prev):
        idx = g * MOE_NB + jnp.minimum(j, MOE_NB - 1)
        e_cur = blk_exp_ref[idx]
        e_prev = blk_exp_ref[jnp.maximum(idx - 1, 0)]

        @pl.when((j == 0) | (e_cur != e_prev))
        def _():
            buf = turn_ref[0]
            for c in weight_copies(e_cur, buf):
                c.wait()
            wgb_ref[...] = wgf_ref[buf].astype(BF16)
            wub_ref[...] = wuf_ref[buf].astype(BF16)
            wdb_ref[...] = wdf_ref[buf].astype(BF16)
            e_next = next_exp_ref[g * N_EXPERTS + e_cur]

            @pl.when(e_next >= 0)
            def _():
                for c in weight_copies(e_next, 1 - buf):
                    c.start()

            turn_ref[0] = 1 - buf

        gather(j + 1, gat_next)
        expert_mlp(gat_cur, yt_cur)
        combine(jnp.maximum(j - 1, 0), yt_prev)

    def step_pair(i, carry):
        step(2 * i, gat0_ref, gat1_ref, yt0_ref, yt1_ref)
        step(2 * i + 1, gat1_ref, gat0_ref, yt1_ref, yt0_ref)
        return carry

    lax.fori_loop(0, (n_used_ref[g] + 2) // 2, step_pair, 0)

    out_copy = pltpu.make_async_copy(acc_vmem, out_hbm.at[g], sems.at[3])
    out_copy.start()
    out_copy.wait()


N_WEIGHT_BUFS = 2


def _expert_call(blk_exp, n_used, next_exp, slots, ws, x_tiles, wg, wu, wd, layer):
    any_spec = pl.BlockSpec(memory_space=pl.ANY)
    return pl.pallas_call(
        functools.partial(_expert_kernel, layer=layer),
        out_shape=jax.ShapeDtypeStruct((N_GROUPS, ACC_TILE_ROWS, LANES), F32),
        grid_spec=pltpu.PrefetchScalarGridSpec(
            num_scalar_prefetch=3,
            grid=(N_GROUPS,),
            in_specs=[any_spec] * 6,
            out_specs=any_spec,
            scratch_shapes=[
                pltpu.VMEM((ACC_TILE_ROWS, LANES), F32),
                pltpu.VMEM((ACC_TILE_ROWS, LANES), F32),
                pltpu.VMEM((MOE_BM * SUBLANES, LANES), F32),
                pltpu.VMEM((MOE_BM * SUBLANES, LANES), F32),
                pltpu.VMEM((MOE_BM * SUBLANES, LANES), F32),
                pltpu.VMEM((MOE_BM * SUBLANES, LANES), F32),
                pltpu.VMEM((N_WEIGHT_BUFS, D, D_EXPERT), F32),
                pltpu.VMEM((N_WEIGHT_BUFS, D, D_EXPERT), F32),
                pltpu.VMEM((N_WEIGHT_BUFS, D_EXPERT, D), F32),
                pltpu.VMEM((D, D_EXPERT), BF16),
                pltpu.VMEM((D, D_EXPERT), BF16),
                pltpu.VMEM((D_EXPERT, D), BF16),
                pltpu.SMEM((SLOT_ROWS, LANES), I32),
                pltpu.SMEM((N_ASSIGN,), F32),
                pltpu.SMEM((1,), I32),
                pltpu.SemaphoreType.DMA((4,)),
                pltpu.SemaphoreType.DMA((N_WEIGHT_BUFS, 3)),
            ]),
        compiler_params=_params(("arbitrary",)),
        name="moe_experts",
    )(blk_exp, n_used, next_exp, slots, ws, x_tiles, wg, wu, wd)


def _finish_kernel(base_ref, routed_ref, ada_ref, oc_ref, ol_ref):
    routed = jnp.concatenate(
        [routed_ref[pl.ds(c, ROW_TILE, stride=LANE_CHUNKS), :] for c in range(LANE_CHUNKS)], axis=1)
    _write_stream(oc_ref, ol_ref, base_ref[...] + ada_ref[5:6, :] * routed)


def _finish_call(base, routed, ada, layer):
    rows = ROW_TILE * LANE_CHUNKS
    return pl.pallas_call(
        _finish_kernel,
        out_shape=STREAM_SHAPES,
        grid=(N_TILES,),
        in_specs=[
            _tile_spec(),
            pl.BlockSpec((None, rows, LANES), lambda i: (i // TILES_PER_GROUP, i % TILES_PER_GROUP, 0)),
            _ada_spec(layer),
        ],
        out_specs=STREAM_SPECS,
        compiler_params=_params(("arbitrary",)),
        name="moe_finish",
    )(base, routed, ada)


BM_SHIFT = MOE_BM.bit_length() - 1


def _plan_kernel(cnt_ref, start_ref, n_used_ref, blk_exp_ref, next_exp_ref):
    for g in range(N_GROUPS):
        def expert(e, carry):
            run, last = carry
            blocks = (cnt_ref[g * N_EXPERTS + e] + MOE_BM - 1) >> BM_SHIFT
            start_ref[g * N_EXPERTS + e] = run << BM_SHIFT

            def mark(b, c):
                blk_exp_ref[g * MOE_NB + b] = e
                return c

            lax.fori_loop(run, run + blocks, mark, 0)
            return run + blocks, jnp.where(blocks > 0, e, last)

        used, last = lax.fori_loop(0, N_EXPERTS, expert, (0, 0))
        n_used_ref[g] = used

        def mark_unused(b, c):
            blk_exp_ref[g * MOE_NB + b] = last
            return c

        lax.fori_loop(used, MOE_NB, mark_unused, 0)

        def link(i, nxt):
            e = N_EXPERTS - 1 - i
            next_exp_ref[g * N_EXPERTS + e] = nxt
            return jnp.where(cnt_ref[g * N_EXPERTS + e] > 0, e, nxt)

        lax.fori_loop(0, N_EXPERTS, link, -1)


def _plan_call(counts):
    smem = pl.BlockSpec(memory_space=pltpu.SMEM)
    return pl.pallas_call(
        _plan_kernel,
        out_shape=(jax.ShapeDtypeStruct((N_GROUPS * N_EXPERTS,), I32),
                   jax.ShapeDtypeStruct((N_GROUPS,), I32),
                   jax.ShapeDtypeStruct((N_GROUPS * MOE_NB,), I32),
                   jax.ShapeDtypeStruct((N_GROUPS * N_EXPERTS,), I32)),
        in_specs=[smem],
        out_specs=(smem, smem, smem, smem),
        name="moe_plan",
    )(counts)


def _moe_layer(y, ada, norm_ffn, router_w, router_bias, wg, wu, wd, sg, su, sd, layer):
    x_tiles, base, keys, ws, cnts = _router_call(
        y, ada, norm_ffn, router_w, router_bias, sg, su, sd, layer)
    starts, n_used, blk_exp, next_exp = _plan_call(cnts[:, :, 0].astype(I32).reshape(-1))
    pos = _pos_call(starts, keys)
    slots = jnp.stack([
        _slot_call(pos[:, g * MOE_GROUP:(g + 1) * MOE_GROUP].reshape(-1))
        for g in range(N_GROUPS)]).reshape(N_GROUPS, SLOT_ROWS, LANES)
    ws_g = ws.reshape(TOP_K, N_GROUPS, MOE_GROUP).transpose(1, 0, 2).reshape(N_GROUPS, N_ASSIGN)
    routed = _expert_call(blk_exp, n_used, next_exp, slots, ws_g, x_tiles, wg, wu, wd, layer)
    return _finish_call(base, routed, ada, layer)


def kernel(x_prompt, x_sample, cache_k, cache_v, c, c_ctx, ada_w, ada_b, norm_mix, norm_ffn,
           conv_w1, conv_b1, conv_dw, conv_dw_b, conv_norm, conv_w2, conv_b2,
           attn_wqkv, attn_q_norm, attn_k_norm, attn_sink, attn_wo,
           router_w, router_bias, exp_w_gate, exp_w_up, exp_w_down,
           sh_w_gate, sh_w_up, sh_w_down):
    y = (x_prompt.reshape(TP, D), x_sample.reshape(TS, D))
    cond = jnp.concatenate(
        [c_ctx[None, :], c, jnp.zeros((COND_ROWS - N_COND, D), F32)], axis=0)
    ada = _ada_call(cond, ada_w, ada_b).reshape(DEPTH, COND_ROWS, N_ADA, D)

    new_k = new_v = None
    for layer in range(DEPTH):
        j = layer // 2
        if layer % 2 == 0:
            u = _conv_in_call(y, ada, norm_mix, conv_w1, conv_b1, layer, j)
            y = _conv_out_call(u, y, ada, conv_dw, conv_dw_b, conv_norm, conv_w2, conv_b2, layer, j)
        else:
            q, k, v = _qkv_call(y, ada, norm_mix, attn_wqkv, attn_q_norm, attn_k_norm, layer, j)
            new_k = k[:TP].reshape(N_CTX_SEQ, 1, CTX_LEN, N_KV, HEAD_DIM)
            new_v = v[:TP].reshape(N_CTX_SEQ, 1, CTX_LEN, N_KV, HEAD_DIM)
            y = (_attn_ctx_call(q, k, v, attn_sink, attn_wo, y[0], ada, layer, j),
                 _attn_lat_call(q, k, v, cache_k, cache_v, attn_sink, attn_wo, y[1], ada, layer, j))
        y = _moe_layer(y, ada, norm_ffn, router_w, router_bias, exp_w_gate, exp_w_up, exp_w_down,
                       sh_w_gate, sh_w_up, sh_w_down, layer)

    y_p = y[0].reshape(N_CTX_SEQ, CTX_LEN, D)
    y_s = y[1].reshape(N_LAT_SEQ, LAT_LEN, D)
    return (y_p, y_s, new_k, new_v)
```

```python
import functools

import jax
import jax.numpy as jnp
import numpy as np
from jax import lax
from jax.experimental import pallas as pl
from jax.experimental.pallas import tpu as pltpu

F32 = jnp.float32
BF16 = jnp.bfloat16
I32 = jnp.int32

D = 1024
N_CTX_SEQ = 32
CTX_LEN = 256
N_LAT_SEQ = 4
LAT_LEN = 1024
TP = N_CTX_SEQ * CTX_LEN
TS = N_LAT_SEQ * LAT_LEN
T = TP + TS
DEPTH = 2
N_ADA = 6
N_COND = 1 + N_LAT_SEQ
COND_ROWS = 8
CONV_WIDTH = 31
CONV_PAD = CONV_WIDTH // 2
N_HEADS = 16
N_KV = 4
HEAD_DIM = 64
GQA = N_HEADS // N_KV
KV_WIDTH = N_KV * HEAD_DIM
QKV_WIDTH = D + 2 * KV_WIDTH
WINDOW = 128
GRID_W = 64
ROPE_PAIRS = HEAD_DIM // 4
ROPE_THETA = 10000.0
N_EXPERTS = 64
N_EGROUPS = 8
EGROUP = N_EXPERTS // N_EGROUPS
TOPK_GROUPS = 4
TOP_K = 8
D_EXPERT = 256
D_SHARED = 256
ROUTED_SCALE = 2.5
NORM_EPS = 1e-6

LANES = 128
SUBLANES = 8
VMEM_LIMIT = 56 * 1024 * 1024

ROW_TILE = 256
N_TILES = T // ROW_TILE
N_CTX_TILES = TP // ROW_TILE
LAT_TILES_PER_SEQ = LAT_LEN // ROW_TILE
HALO = 16
LANE_CHUNKS = D // LANES

MOE_GROUP = 4096
N_GROUPS = T // MOE_GROUP
N_CTX_GROUPS = TP // MOE_GROUP
TILES_PER_GROUP = MOE_GROUP // ROW_TILE
MOE_BM = 256
MOE_NB = MOE_GROUP * TOP_K // MOE_BM + N_EXPERTS
N_ASSIGN = MOE_GROUP * TOP_K
ACC_ROWS = MOE_GROUP + SUBLANES
RANK_BITS = 12
Q_BLOCK = 128


def _cond_of_tile(i):
    return jnp.where(i < N_CTX_TILES, 0, 1 + (i - N_CTX_TILES) // LAT_TILES_PER_SEQ)


def _params(sem, vmem=VMEM_LIMIT):
    return pltpu.CompilerParams(dimension_semantics=sem, vmem_limit_bytes=vmem)


def _bdot(a, b):
    return jnp.dot(a.astype(BF16), b.astype(BF16), preferred_element_type=F32)


def _split(a):
    hi = a.astype(BF16)
    lo = (a - hi.astype(F32)).astype(BF16)
    return hi, lo


def _dot3(a, b):
    a_hi, a_lo = _split(a)
    b_hi, b_lo = _split(b)
    d = functools.partial(jnp.dot, preferred_element_type=F32)
    return d(a_hi, b_hi) + d(a_lo, b_hi) + d(a_hi, b_lo)


def _dot2(a, b_bf16):
    a_hi, a_lo = _split(a)
    d = functools.partial(jnp.dot, preferred_element_type=F32)
    return d(a_hi, b_bf16) + d(a_lo, b_bf16)


def _rms(x, g):
    return x * lax.rsqrt(jnp.mean(x * x, axis=-1, keepdims=True) + NORM_EPS) * g


def _modulate(x, g, shift, scale):
    return _rms(x, g) * (1.0 + scale) + shift


def _silu(x):
    return x * jax.nn.sigmoid(x)


def _ada_spec(layer):
    return pl.BlockSpec((None, None, N_ADA, D), lambda i: (layer, _cond_of_tile(i), 0, 0))


def _tile_spec(width=D):
    return pl.BlockSpec((ROW_TILE, width), lambda i: (i, 0))


STREAM_SPECS = [
    pl.BlockSpec((ROW_TILE, D), lambda i: (jnp.minimum(i, N_CTX_TILES - 1), 0)),
    pl.BlockSpec((ROW_TILE, D), lambda i: (jnp.maximum(i - N_CTX_TILES, 0), 0)),
]
STREAM_SHAPES = (jax.ShapeDtypeStruct((TP, D), F32), jax.ShapeDtypeStruct((TS, D), F32))


def _read_stream(ctx_ref, lat_ref):
    return jnp.where(pl.program_id(0) < N_CTX_TILES, ctx_ref[...], lat_ref[...])


def _write_stream(ctx_ref, lat_ref, value):
    i = pl.program_id(0)

    @pl.when(i < N_CTX_TILES)
    def _():
        ctx_ref[...] = value

    @pl.when(i >= N_CTX_TILES)
    def _():
        lat_ref[...] = value


def _const_spec(shape):
    nd = len(shape)
    return pl.BlockSpec(shape, lambda *_: (0,) * nd)


def _layer_spec(layer, shape):
    nd = len(shape)
    return pl.BlockSpec((None,) + tuple(shape), lambda *_: (layer,) + (0,) * nd)


ADA_NB = 512


def _ada_kernel(c_ref, w_ref, b_ref, o_ref):
    o_ref[...] = _dot3(_silu(c_ref[...]), w_ref[...]) + b_ref[...]


def _ada_call(cond, ada_w, ada_b):
    return pl.pallas_call(
        _ada_kernel,
        out_shape=jax.ShapeDtypeStruct((DEPTH, COND_ROWS, N_ADA * D), F32),
        grid=(DEPTH, N_ADA * D // ADA_NB),
        in_specs=[
            pl.BlockSpec((COND_ROWS, D), lambda l, n: (0, 0)),
            pl.BlockSpec((None, D, ADA_NB), lambda l, n: (l, 0, n)),
            pl.BlockSpec((None, 1, ADA_NB), lambda l, n: (l, 0, n)),
        ],
        out_specs=pl.BlockSpec((None, COND_ROWS, ADA_NB), lambda l, n: (l, 0, n)),
        compiler_params=_params(("arbitrary", "arbitrary")),
        name="ada_params",
    )(cond, ada_w, ada_b.reshape(DEPTH, 1, N_ADA * D))


def _conv_in_kernel(xc_ref, xl_ref, ada_ref, g_ref, w1_ref, b1_ref, u_ref, w1b_ref):
    @pl.when(pl.program_id(0) == 0)
    def _():
        w1b_ref[...] = w1_ref[...].astype(BF16)

    h = _modulate(_read_stream(xc_ref, xl_ref), g_ref[...], ada_ref[0:1, :], ada_ref[1:2, :])
    u = jnp.dot(h.astype(BF16), w1b_ref[...], preferred_element_type=F32) + b1_ref[...]
    u_ref[...] = u[:, :D] * jax.nn.sigmoid(u[:, D:])


def _conv_in_call(y, ada, norm_mix, conv_w1, conv_b1, layer, j):
    return pl.pallas_call(
        _conv_in_kernel,
        out_shape=jax.ShapeDtypeStruct((T, D), F32),
        grid=(N_TILES,),
        in_specs=STREAM_SPECS + [
            _ada_spec(layer),
            _layer_spec(layer, (1, D)),
            _layer_spec(j, (D, 2 * D)),
            _layer_spec(j, (1, 2 * D)),
        ],
        out_specs=_tile_spec(),
        scratch_shapes=[pltpu.VMEM((D, 2 * D), BF16)],
        compiler_params=_params(("arbitrary",)),
        name="conv_in",
    )(*y, ada, norm_mix.reshape(DEPTH, 1, D), conv_w1, conv_b1.reshape(-1, 1, 2 * D))


CONV_ROWS = 64
BUF_ROWS = ROW_TILE + 2 * HALO


def _conv_out_kernel(u_ref, up_ref, un_ref, yc_ref, yl_ref, ada_ref, dw_ref, dwb_ref, gn_ref,
                     w2_ref, b2_ref, oc_ref, ol_ref, buf_ref, z_ref, w2b_ref):
    i = pl.program_id(0)

    @pl.when(i == 0)
    def _():
        w2b_ref[...] = w2_ref[...].astype(BF16)

    lat = i >= N_CTX_TILES
    pos = jnp.where(lat, i - N_CTX_TILES, 0) % LAT_TILES_PER_SEQ
    has_prev = lat & (pos != 0)
    has_next = lat & (pos != LAT_TILES_PER_SEQ - 1)
    for c in range(LANE_CHUNKS):
        cs = slice(c * LANES, (c + 1) * LANES)
        buf_ref[c, 0:HALO, :] = jnp.where(has_prev, up_ref[:, cs], 0.0)
        buf_ref[c, HALO:HALO + ROW_TILE, :] = u_ref[:, cs]
        buf_ref[c, HALO + ROW_TILE:BUF_ROWS, :] = jnp.where(has_next, un_ref[:, cs], 0.0)

    off = HALO - CONV_PAD
    for c in range(LANE_CHUNKS):
        cs = slice(c * LANES, (c + 1) * LANES)
        for r0 in range(0, ROW_TILE, CONV_ROWS):
            acc = jnp.broadcast_to(dwb_ref[:, cs], (CONV_ROWS, LANES))
            for k in range(CONV_WIDTH):
                win = buf_ref[c, r0 + k + off:r0 + k + off + CONV_ROWS, :]
                acc = acc + dw_ref[k:k + 1, cs] * win
            z_ref[r0:r0 + CONV_ROWS, cs] = acc

    z = _silu(_rms(z_ref[...], gn_ref[...]))
    m = jnp.dot(z.astype(BF16), w2b_ref[...], preferred_element_type=F32) + b2_ref[...]
    _write_stream(oc_ref, ol_ref, _read_stream(yc_ref, yl_ref) + ada_ref[2:3, :] * m)


def _conv_out_call(u, y, ada, conv_dw, conv_dw_b, conv_norm, conv_w2, conv_b2, layer, j):
    halos_per_tile = ROW_TILE // HALO
    last_halo = T // HALO - 1
    return pl.pallas_call(
        _conv_out_kernel,
        out_shape=STREAM_SHAPES,
        grid=(N_TILES,),
        in_specs=[
            _tile_spec(),
            pl.BlockSpec((HALO, D), lambda i: (jnp.maximum(i * halos_per_tile - 1, 0), 0)),
            pl.BlockSpec((HALO, D), lambda i: (jnp.minimum((i + 1) * halos_per_tile, last_halo), 0)),
        ] + STREAM_SPECS + [
            _ada_spec(layer),
            _layer_spec(j, (CONV_WIDTH + 1, D)),
            _layer_spec(j, (1, D)),
            _layer_spec(j, (1, D)),
            _layer_spec(j, (D, D)),
            _layer_spec(j, (1, D)),
        ],
        out_specs=STREAM_SPECS,
        scratch_shapes=[pltpu.VMEM((LANE_CHUNKS, BUF_ROWS, LANES), F32), pltpu.VMEM((ROW_TILE, D), F32),
                        pltpu.VMEM((D, D), BF16)],
        compiler_params=_params(("arbitrary",)),
        name="conv_out",
    )(u, u, u, *y, ada, jnp.pad(conv_dw, ((0, 0), (0, 1), (0, 0))), conv_dw_b.reshape(-1, 1, D),
      conv_norm.reshape(-1, 1, D), conv_w2, conv_b2.reshape(-1, 1, D))


def _swap_pairs(x, width):
    lane = lax.broadcasted_iota(I32, x.shape, 1)
    first = (lane % (2 * ROPE_PAIRS)) < ROPE_PAIRS
    return jnp.where(first, pltpu.roll(x, width - ROPE_PAIRS, 1), pltpu.roll(x, ROPE_PAIRS, 1))


def _qkv_kernel(xc_ref, xl_ref, ada_ref, g_ref, w_ref, qn_ref, kn_ref, cos_ref, sin_ref,
                q_ref, k_ref, v_ref, kt_ref, vt_ref, wb_ref, hs_ref):
    i = pl.program_id(0)

    @pl.when(i == 0)
    def _():
        wb_ref[...] = w_ref[...].astype(BF16)
        r = lax.broadcasted_iota(I32, (D, D), 0) // HEAD_DIM
        c = lax.broadcasted_iota(I32, (D, D), 1) // HEAD_DIM
        hs_ref[...] = jnp.where(r == c, 1.0 / HEAD_DIM, 0.0).astype(BF16)

    h = _modulate(_read_stream(xc_ref, xl_ref), g_ref[...], ada_ref[0:1, :], ada_ref[1:2, :])
    qkv = jnp.dot(h.astype(BF16), wb_ref[...], preferred_element_type=F32)
    q = qkv[:, :D]
    k = qkv[:, D:D + KV_WIDTH]
    v_ref[...] = qkv[:, D + KV_WIDTH:]
    q_ms = _dot2(q * q, hs_ref[...])
    k_ms = _dot2(k * k, hs_ref[0:KV_WIDTH, 0:KV_WIDTH])
    qn = q * lax.rsqrt(q_ms + NORM_EPS) * qn_ref[...]
    kn = k * lax.rsqrt(k_ms + NORM_EPS) * kn_ref[...]

    @pl.when(i < N_CTX_TILES)
    def _():
        q_ref[...] = qn
        k_ref[...] = kn
        kt_ref[...] = kn.T
        vt_ref[...] = qkv[:, D + KV_WIDTH:].T

    @pl.when(i >= N_CTX_TILES)
    def _():
        cos = cos_ref[...]
        sin = sin_ref[...]
        q_ref[...] = qn * cos + _swap_pairs(qn, D) * sin
        k_ref[...] = kn * cos[:, :KV_WIDTH] + _swap_pairs(kn, KV_WIDTH) * sin[:, :KV_WIDTH]


def _rope_tables():
    pos = np.arange(LAT_LEN)
    row = (pos // GRID_W).astype(np.float32)
    col = (pos % GRID_W).astype(np.float32)
    inv_freq = np.float32(ROPE_THETA) ** (-np.arange(ROPE_PAIRS, dtype=np.float32) / ROPE_PAIRS)
    d = np.arange(D) % HEAD_DIM
    freq = inv_freq[d % ROPE_PAIRS].astype(np.float32)
    p = np.where((d >= HEAD_DIM // 2)[None, :], col[:, None], row[:, None])
    ang = (p * freq[None, :]).astype(np.float32)
    first = (d % (2 * ROPE_PAIRS)) < ROPE_PAIRS
    cos = np.cos(ang).astype(np.float32)
    sin = np.sin(ang).astype(np.float32)
    return jnp.asarray(cos), jnp.asarray(np.where(first[None, :], -sin, sin))


def _qkv_call(y, ada, norm_mix, wqkv, q_norm, k_norm, layer, j):
    cos, sin = _rope_tables()
    table_spec = pl.BlockSpec(
        (ROW_TILE, D), lambda i: (jnp.maximum(i - N_CTX_TILES, 0) % LAT_TILES_PER_SEQ, 0))
    cache_spec = pl.BlockSpec((None, KV_WIDTH, CTX_LEN), lambda i: (jnp.minimum(i, N_CTX_SEQ - 1), 0, 0))
    return pl.pallas_call(
        _qkv_kernel,
        out_shape=(jax.ShapeDtypeStruct((T, D), F32), jax.ShapeDtypeStruct((T, KV_WIDTH), F32),
                   jax.ShapeDtypeStruct((T, KV_WIDTH), F32),
                   jax.ShapeDtypeStruct((N_CTX_SEQ, KV_WIDTH, CTX_LEN), F32),
                   jax.ShapeDtypeStruct((N_CTX_SEQ, KV_WIDTH, CTX_LEN), F32)),
        grid=(N_TILES,),
        in_specs=STREAM_SPECS + [
            _ada_spec(layer),
            _layer_spec(layer, (1, D)),
            _layer_spec(j, (D, QKV_WIDTH)),
            _const_spec((1, D)),
            _const_spec((1, KV_WIDTH)),
            table_spec,
            table_spec,
        ],
        out_specs=(_tile_spec(), _tile_spec(KV_WIDTH), _tile_spec(KV_WIDTH), cache_spec, cache_spec),
        scratch_shapes=[pltpu.VMEM((D, QKV_WIDTH), BF16), pltpu.VMEM((D, D), BF16)],
        compiler_params=_params(("arbitrary",)),
        name="qkv",
    )(*y, ada, norm_mix.reshape(DEPTH, 1, D), wqkv,
      jnp.tile(q_norm[j], N_HEADS).reshape(1, D), jnp.tile(k_norm[j], N_KV).reshape(1, KV_WIDTH),
      cos, sin)


HEADS_PER_TILE = LANES // HEAD_DIM
TILES_PER_KV = GQA // HEADS_PER_TILE


def _gqa_attention(q, segments, sink_ref, rows):
    d = functools.partial(jnp.dot, preferred_element_type=F32)
    half = lax.broadcasted_iota(I32, (rows, LANES), 1) < HEAD_DIM
    q_zero = jnp.zeros((rows, LANES), BF16)

    scores, sinks, kvs = [], [], []
    for kv in range(N_KV):
        sl = slice(kv * HEAD_DIM, (kv + 1) * HEAD_DIM)
        parts = []
        for t in range(TILES_PER_KV):
            tile = kv * TILES_PER_KV + t
            qt = q[:, tile * LANES:(tile + 1) * LANES]
            parts += [jnp.where(half, qt, q_zero), jnp.where(half, q_zero, qt)]
        qg = jnp.concatenate(parts, axis=0)
        seg_scores, seg_kv = [], []
        for k, v, bias in segments:
            kk = jnp.concatenate([k[:, sl], k[:, sl]], axis=1)
            s = lax.dot_general(qg, kk, (((1,), (1,)), ((), ())),
                                preferred_element_type=F32) * (HEAD_DIM ** -0.5)
            seg_scores.append(s if bias is None else s + bias)
            v_zero = jnp.zeros_like(v[:, sl])
            seg_kv.append((jnp.concatenate([v[:, sl], v_zero], axis=1),
                           jnp.concatenate([v_zero, v[:, sl]], axis=1)))
        scores.append(seg_scores)
        kvs.append(seg_kv)
        sinks.append(jnp.concatenate(
            [jnp.broadcast_to(sink_ref[0:1, kv * GQA + g:kv * GQA + g + 1], (rows, 1))
             for g in range(GQA)], axis=0))

    maxes = []
    for kv in range(N_KV):
        m = sinks[kv]
        for s in scores[kv]:
            m = jnp.maximum(m, jnp.max(s, axis=1, keepdims=True))
        maxes.append(m)
    probs = [[jnp.exp(s - maxes[kv]).astype(BF16) for s in scores[kv]] for kv in range(N_KV)]

    tiles = []
    for kv in range(N_KV):
        den = jnp.exp(sinks[kv] - maxes[kv])
        for p in probs[kv]:
            den = den + d(p, jnp.ones((p.shape[1], LANES), BF16))
        inv = 1.0 / den
        for t in range(TILES_PER_KV):
            tile = None
            for h in range(HEADS_PER_TILE):
                g = t * HEADS_PER_TILE + h
                rs = slice(g * rows, (g + 1) * rows)
                o = None
                for p, v_halves in zip(probs[kv], kvs[kv]):
                    part = d(p[rs, :], v_halves[h])
                    o = part if o is None else o + part
                o = o * inv[rs, :]
                tile = o if tile is None else tile + o
            tiles.append(tile)
    return jnp.concatenate(tiles, axis=1)


def _attn_ctx_kernel(q_ref, k_ref, v_ref, sink_ref, wo_ref, y_ref, ada_ref, o_ref, wob_ref):
    @pl.when(pl.program_id(0) == 0)
    def _():
        wob_ref[...] = wo_ref[...].astype(BF16)

    segments = [(k_ref[...].astype(BF16), v_ref[...].astype(BF16), None)]
    o = _gqa_attention(q_ref[...].astype(BF16), segments, sink_ref, CTX_LEN)
    m_out = jnp.dot(o.astype(BF16), wob_ref[...], preferred_element_type=F32)
    o_ref[...] = y_ref[...] + ada_ref[2:3, :] * m_out


def _attn_ctx_call(q, k, v, sink, wo, y_ctx, ada, layer, j):
    return pl.pallas_call(
        _attn_ctx_kernel,
        out_shape=STREAM_SHAPES[0],
        grid=(N_CTX_SEQ,),
        in_specs=[
            _tile_spec(), _tile_spec(KV_WIDTH), _tile_spec(KV_WIDTH),
            _layer_spec(j, (1, N_HEADS)),
            _layer_spec(j, (D, D)),
            _tile_spec(),
            _ada_spec(layer),
        ],
        out_specs=_tile_spec(),
        scratch_shapes=[pltpu.VMEM((D, D), BF16)],
        compiler_params=_params(("arbitrary",)),
        name="attn_ctx",
    )(q, k, v, sink.reshape(-1, 1, N_HEADS), wo, y_ctx, ada)


LAT_WIN = 3 * WINDOW
N_QBLOCKS = LAT_LEN // Q_BLOCK


def _attn_lat_kernel(q_ref, k_ref, v_ref, kc_ref, vc_ref, sink_ref, wo_ref, y_ref, ada_ref,
                     o_ref, wob_ref):
    b = pl.program_id(0)
    n = pl.program_id(1)

    @pl.when((b == 0) & (n == 0))
    def _():
        wob_ref[...] = wo_ref[...].astype(BF16)

    start = pl.multiple_of(jnp.clip((n - 1) * WINDOW, 0, LAT_LEN - LAT_WIN), WINDOW)
    kw = k_ref[pl.ds(start, LAT_WIN), :].astype(BF16)
    vw = v_ref[pl.ds(start, LAT_WIN), :].astype(BF16)
    qpos = n * Q_BLOCK + lax.broadcasted_iota(I32, (GQA * Q_BLOCK, LAT_WIN), 0) % Q_BLOCK
    kpos = start + lax.broadcasted_iota(I32, (GQA * Q_BLOCK, LAT_WIN), 1)
    band = jnp.where(jnp.abs(kpos - qpos) <= WINDOW, 0.0, -jnp.inf)
    segments = [(kw, vw, band),
                (kc_ref[...].astype(BF16), vc_ref[...].astype(BF16), None)]
    o = _gqa_attention(q_ref[...].astype(BF16), segments, sink_ref, Q_BLOCK)
    m_out = jnp.dot(o.astype(BF16), wob_ref[...], preferred_element_type=F32)
    o_ref[...] = y_ref[...] + ada_ref[2:3, :] * m_out


def _attn_lat_call(q, k, v, cache_k, cache_v, sink, wo, y_lat, ada, layer, j):
    q_row0 = TP // Q_BLOCK
    seq0 = TP // LAT_LEN
    qspec = pl.BlockSpec((Q_BLOCK, D), lambda b, n: (q_row0 + b * N_QBLOCKS + n, 0))
    yspec = pl.BlockSpec((Q_BLOCK, D), lambda b, n: (b * N_QBLOCKS + n, 0))
    kvspec = pl.BlockSpec((LAT_LEN, KV_WIDTH), lambda b, n: (seq0 + b, 0))
    cspec = pl.BlockSpec((None, None, CTX_LEN, KV_WIDTH), lambda b, n: (b, j, 0, 0))
    ck = cache_k.reshape(N_LAT_SEQ, -1, CTX_LEN, KV_WIDTH)
    cv = cache_v.reshape(N_LAT_SEQ, -1, CTX_LEN, KV_WIDTH)
    return pl.pallas_call(
        _attn_lat_kernel,
        out_shape=STREAM_SHAPES[1],
        grid=(N_LAT_SEQ, N_QBLOCKS),
        in_specs=[
            qspec, kvspec, kvspec, cspec, cspec,
            _layer_spec(j, (1, N_HEADS)),
            _layer_spec(j, (D, D)),
            yspec,
            pl.BlockSpec((None, None, N_ADA, D), lambda b, n: (layer, 1 + b, 0, 0)),
        ],
        out_specs=yspec,
        scratch_shapes=[pltpu.VMEM((D, D), BF16)],
        compiler_params=_params(("arbitrary", "arbitrary")),
        name="attn_lat",
    )(q, k, v, ck, cv, sink.reshape(-1, 1, N_HEADS), wo, y_lat, ada)


def _router_kernel(yc_ref, yl_ref, ada_ref, g_ref, rw_ref, rb_ref, sg_ref, su_ref, sd_ref,
                   h_ref, base_ref, key_ref, w_ref, cnt_ref,
                   carry_ref, tri_ref, sgb_ref, sub_ref, sdb_ref):
    i = pl.program_id(0)

    @pl.when(i == 0)
    def _():
        sgb_ref[...] = sg_ref[...].astype(BF16)
        sub_ref[...] = su_ref[...].astype(BF16)
        sdb_ref[...] = sd_ref[...].astype(BF16)
        r = lax.broadcasted_iota(I32, (ROW_TILE, ROW_TILE), 0)
        c = lax.broadcasted_iota(I32, (ROW_TILE, ROW_TILE), 1)
        tri_ref[...] = jnp.where(r < c, 1.0, 0.0).astype(BF16)

    @pl.when(i % TILES_PER_GROUP == 0)
    def _():
        carry_ref[...] = jnp.zeros_like(carry_ref)

    y = _read_stream(yc_ref, yl_ref)
    h = _modulate(y, g_ref[...], ada_ref[3:4, :], ada_ref[4:5, :])
    for c in range(LANE_CHUNKS):
        h_ref[pl.ds(c, ROW_TILE, stride=LANE_CHUNKS), :] = h[:, c * LANES:(c + 1) * LANES]
    hb = h.astype(BF16)

    d = functools.partial(jnp.dot, preferred_element_type=F32)
    act = (_silu(d(hb, sgb_ref[...])) * d(hb, sub_ref[...])).astype(BF16)
    base_ref[...] = y + ada_ref[5:6, :] * d(act, sdb_ref[...])

    logits = _dot3(h, rw_ref[...])
    logits_t = jnp.concatenate([logits, jnp.zeros_like(logits)], axis=1).T[:N_EXPERTS]
    scores = jax.nn.sigmoid(logits_t)
    shape3 = (N_EGROUPS, EGROUP, ROW_TILE)
    scores3 = scores.reshape(shape3)
    choice3 = (scores + rb_ref[...]).reshape(shape3)
    sub = lax.broadcasted_iota(I32, shape3, 1)
    eidx = lax.broadcasted_iota(I32, shape3, 0) * EGROUP + sub
    gidx = lax.broadcasted_iota(I32, (N_EGROUPS, 1, ROW_TILE), 0)
    neg = -jnp.inf

    m1 = jnp.max(choice3, axis=1, keepdims=True)
    i1 = jnp.min(jnp.where(choice3 == m1, sub, EGROUP), axis=1, keepdims=True)
    m2 = jnp.max(jnp.where(sub == i1, neg, choice3), axis=1, keepdims=True)
    gscore = m1 + m2
    allowed = jnp.zeros((N_EGROUPS, 1, ROW_TILE), jnp.bool_)
    for _ in range(TOPK_GROUPS):
        gm = jnp.max(gscore, axis=0, keepdims=True)
        gi = jnp.min(jnp.where(gscore == gm, gidx, N_EGROUPS), axis=0, keepdims=True)
        hit = gidx == gi
        allowed = allowed | hit
        gscore = jnp.where(hit, neg, gscore)

    def reduce_experts(fn, x):
        return fn(fn(x, axis=0, keepdims=True), axis=1, keepdims=True)

    cm = jnp.where(allowed, choice3, neg)
    picked = jnp.zeros(shape3, F32)
    e_rows, s_rows = [], []
    for _ in range(TOP_K):
        m = reduce_experts(jnp.max, cm)
        ik = reduce_experts(jnp.min, jnp.where(cm == m, eidx, N_EXPERTS))
        hit = eidx == ik
        e_rows.append(ik)
        s_rows.append(reduce_experts(jnp.sum, jnp.where(hit, scores3, 0.0)))
        cm = jnp.where(hit, neg, cm)
        picked = jnp.where(hit, 1.0, picked)

    picked2 = picked.reshape(N_EXPERTS, ROW_TILE)
    cum3 = (d(picked2.astype(BF16), tri_ref[...]) + carry_ref[...]).reshape(shape3)
    carry_ref[...] = carry_ref[...] + jnp.sum(picked2, axis=1, keepdims=True)
    cnt_ref[...] = jnp.broadcast_to(carry_ref[...], (N_EXPERTS, LANES))

    denom = functools.reduce(lambda a, b: a + b, s_rows)
    key_rows, w_rows = [], []
    for kk in range(TOP_K):
        rank = reduce_experts(jnp.sum, jnp.where(eidx == e_rows[kk], cum3, 0.0)).astype(I32)
        key_rows.append(((e_rows[kk] << RANK_BITS) | rank).reshape(1, ROW_TILE))
        w_rows.append((s_rows[kk] / denom * ROUTED_SCALE).reshape(1, ROW_TILE))
    key_ref[...] = jnp.concatenate(key_rows, axis=0)
    w_ref[...] = jnp.concatenate(w_rows, axis=0)


def _router_call(y, ada, norm_ffn, router_w, router_bias, sg, su, sd, layer):
    return pl.pallas_call(
        _router_kernel,
        out_shape=(
            jax.ShapeDtypeStruct((N_GROUPS, MOE_GROUP * LANE_CHUNKS, LANES), F32),
            jax.ShapeDtypeStruct((T, D), F32),
            jax.ShapeDtypeStruct((TOP_K, T), I32),
            jax.ShapeDtypeStruct((TOP_K, T), F32),
            jax.ShapeDtypeStruct((N_GROUPS, N_EXPERTS, LANES), F32),
        ),
        grid=(N_TILES,),
        in_specs=STREAM_SPECS + [
            _ada_spec(layer),
            _layer_spec(layer, (1, D)),
            _layer_spec(layer, (D, N_EXPERTS)),
            _layer_spec(layer, (N_EXPERTS, 1)),
            _layer_spec(layer, (D, D_SHARED)),
            _layer_spec(layer, (D, D_SHARED)),
            _layer_spec(layer, (D_SHARED, D)),
        ],
        out_specs=(
            pl.BlockSpec((None, ROW_TILE * LANE_CHUNKS, LANES),
                         lambda i: (i // TILES_PER_GROUP, i % TILES_PER_GROUP, 0)),
            _tile_spec(),
            pl.BlockSpec((TOP_K, ROW_TILE), lambda i: (0, i)),
            pl.BlockSpec((TOP_K, ROW_TILE), lambda i: (0, i)),
            pl.BlockSpec((None, N_EXPERTS, LANES), lambda i: (i // TILES_PER_GROUP, 0, 0)),
        ),
        scratch_shapes=[
            pltpu.VMEM((N_EXPERTS, 1), F32),
            pltpu.VMEM((ROW_TILE, ROW_TILE), BF16),
            pltpu.VMEM((D, D_SHARED), BF16),
            pltpu.VMEM((D, D_SHARED), BF16),
            pltpu.VMEM((D_SHARED, D), BF16),
        ],
        compiler_params=_params(("arbitrary",)),
        name="router",
    )(*y, ada, norm_ffn.reshape(DEPTH, 1, D), router_w, router_bias.reshape(DEPTH, N_EXPERTS, 1),
      sg, su, sd)


SLOT_WORDS = (MOE_NB + 3) * MOE_BM
SLOT_ROWS = SLOT_WORDS // LANES


def _pos_kernel(start_ref, key_ref, pos_ref):
    g = pl.program_id(0)
    key = key_ref[...]
    e = key >> RANK_BITS
    pos = key & ((1 << RANK_BITS) - 1)
    for ee in range(N_EXPERTS):
        pos = pos + jnp.where(e == ee, start_ref[g * N_EXPERTS + ee], 0)
    pos_ref[...] = pos


def _pos_call(starts, keys):
    spec = pl.BlockSpec((TOP_K, MOE_GROUP), lambda g, s: (0, g))
    return pl.pallas_call(
        _pos_kernel,
        out_shape=jax.ShapeDtypeStruct((TOP_K, T), I32),
        grid_spec=pltpu.PrefetchScalarGridSpec(
            num_scalar_prefetch=1, grid=(N_GROUPS,), in_specs=[spec], out_specs=spec),
        compiler_params=_params(("arbitrary",)),
        name="moe_pos",
    )(starts, keys)


SLOT_ROW_BITS = 16
DUMMY_SLOT = MOE_GROUP * SUBLANES
SLOT_UNROLL = 4


def _slot_kernel(pos_ref, init_hbm, slot_ref, sem):
    fill = pltpu.make_async_copy(init_hbm, slot_ref, sem)
    fill.start()
    fill.wait()
    per_token = SUBLANES + (1 << SLOT_ROW_BITS)

    def place(t, carry):
        word = t * per_token
        for k in range(TOP_K):
            slot_ref[pos_ref[k * MOE_GROUP + t]] = word + ((k * MOE_GROUP) << SLOT_ROW_BITS)
        return carry

    lax.fori_loop(0, MOE_GROUP, place, 0, unroll=SLOT_UNROLL)


def _slot_call(pos):
    smem = pl.BlockSpec(memory_space=pltpu.SMEM)
    return pl.pallas_call(
        _slot_kernel,
        out_shape=jax.ShapeDtypeStruct((SLOT_WORDS,), I32),
        in_specs=[smem, pl.BlockSpec(memory_space=pl.ANY)],
        out_specs=smem,
        scratch_shapes=[pltpu.SemaphoreType.DMA],
        name="moe_slots",
    )(pos, jnp.full((SLOT_WORDS,), DUMMY_SLOT, I32))


TILE_ROWS = MOE_GROUP * SUBLANES
ACC_TILE_ROWS = ACC_ROWS * SUBLANES
RMW_ROWS = 16


def _expert_kernel(blk_exp_ref, n_used_ref, next_exp_ref,
                   slot_hbm, w_hbm, x_hbm, wg_hbm, wu_hbm, wd_hbm, base_hbm, gate_ref,
                   yc_hbm, yl_hbm,
                   x_vmem, acc_vmem, gat0_ref, gat1_ref, yt0_ref, yt1_ref,
                   wgf_ref, wuf_ref, wdf_ref, wgb_ref, wub_ref, wdb_ref, bbuf_ref, ybuf_ref,
                   slot_smem, w_smem, turn_ref, sems, wsems, fsems, *, layer):
    g = pl.program_id(0)
    rows_per_block = MOE_BM // LANES

    def slot_word(block, r):
        return slot_smem[block * rows_per_block + r // LANES, r % LANES]

    def tile_rows(word):
        return pl.ds(pl.multiple_of(word & ((1 << SLOT_ROW_BITS) - 1), SUBLANES), SUBLANES)

    def gather(block, gat_ref):
        for r in range(MOE_BM):
            gat_ref[pl.ds(r * SUBLANES, SUBLANES), :] = x_vmem[tile_rows(slot_word(block, r)), :]

    def expert_mlp(gat_ref, yt_ref):
        xs = jnp.concatenate(
            [gat_ref[pl.ds(c, MOE_BM, stride=LANE_CHUNKS), :] for c in range(LANE_CHUNKS)],
            axis=1).astype(BF16)
        d = functools.partial(jnp.dot, preferred_element_type=F32)
        act = (_silu(d(xs, wgb_ref[...])) * d(xs, wub_ref[...])).astype(BF16)
        yb = d(act, wdb_ref[...])
        for c in range(LANE_CHUNKS):
            yt_ref[pl.ds(c, MOE_BM, stride=LANE_CHUNKS), :] = yb[:, c * LANES:(c + 1) * LANES]

    def combine(block, yt_ref):
        for r0 in range(0, MOE_BM, RMW_ROWS):
            words = [slot_word(block, r) for r in range(r0, r0 + RMW_ROWS)]
            new = [acc_vmem[tile_rows(word), :]
                   + w_smem[word >> SLOT_ROW_BITS] * yt_ref[pl.ds(r * SUBLANES, SUBLANES), :]
                   for r, word in zip(range(r0, r0 + RMW_ROWS), words)]
            for word, tile in zip(words, new):
                acc_vmem[tile_rows(word), :] = tile

    def group_copies():
        return (pltpu.make_async_copy(x_hbm.at[g], x_vmem.at[pl.ds(0, TILE_ROWS)], sems.at[0]),
                pltpu.make_async_copy(slot_hbm.at[g], slot_smem, sems.at[1]),
                pltpu.make_async_copy(w_hbm.at[g], w_smem, sems.at[2]))

    def weight_copies(e, buf):
        return (pltpu.make_async_copy(wg_hbm.at[layer, e], wgf_ref.at[buf], wsems.at[buf, 0]),
                pltpu.make_async_copy(wu_hbm.at[layer, e], wuf_ref.at[buf], wsems.at[buf, 1]),
                pltpu.make_async_copy(wd_hbm.at[layer, e], wdf_ref.at[buf], wsems.at[buf, 2]))

    for c in group_copies():
        c.start()
    turn_ref[0] = 0
    for c in weight_copies(blk_exp_ref[g * MOE_NB], 0):
        c.start()
    acc_vmem[...] = jnp.zeros_like(acc_vmem)
    x_vmem[pl.ds(TILE_ROWS, ACC_TILE_ROWS - TILE_ROWS), :] = jnp.zeros(
        (ACC_TILE_ROWS - TILE_ROWS, LANES), F32)
    yt1_ref[...] = jnp.zeros_like(yt1_ref)
    for c in group_copies():
        c.wait()
    gather(0, gat0_ref)

    def step(j, gat_cur, gat_next, yt_cur, yt_prev):
        idx = g * MOE_NB + jnp.minimum(j, MOE_NB - 1)
        e_cur = blk_exp_ref[idx]
        e_prev = blk_exp_ref[jnp.maximum(idx - 1, 0)]

        @pl.when((j == 0) | (e_cur != e_prev))
        def _():
            buf = turn_ref[0]
            for c in weight_copies(e_cur, buf):
                c.wait()
            wgb_ref[...] = wgf_ref[buf].astype(BF16)
            wub_ref[...] = wuf_ref[buf].astype(BF16)
            wdb_ref[...] = wdf_ref[buf].astype(BF16)
            e_next = next_exp_ref[g * N_EXPERTS + e_cur]

            @pl.when(e_next >= 0)
            def _():
                for c in weight_copies(e_next, 1 - buf):
                    c.start()

            turn_ref[0] = 1 - buf

        gather(j + 1, gat_next)
        expert_mlp(gat_cur, yt_cur)
        combine(jnp.maximum(j - 1, 0), yt_prev)

    def step_pair(i, carry):
        step(2 * i, gat0_ref, gat1_ref, yt0_ref, yt1_ref)
        step(2 * i + 1, gat1_ref, gat0_ref, yt1_ref, yt0_ref)
        return carry

    lax.fori_loop(0, (n_used_ref[g] + 2) // 2, step_pair, 0)

    tile0 = g * TILES_PER_GROUP
    in_ctx = g < N_CTX_GROUPS

    def base_copy(l, buf):
        return pltpu.make_async_copy(
            base_hbm.at[pl.ds((tile0 + l) * ROW_TILE, ROW_TILE)], bbuf_ref.at[buf], fsems.at[0, buf])

    def ctx_copy(l, buf):
        return pltpu.make_async_copy(
            ybuf_ref.at[buf], yc_hbm.at[pl.ds((tile0 + l) * ROW_TILE, ROW_TILE)], fsems.at[1, buf])

    def lat_copy(l, buf):
        return pltpu.make_async_copy(
            ybuf_ref.at[buf], yl_hbm.at[pl.ds((tile0 + l - N_CTX_TILES) * ROW_TILE, ROW_TILE)],
            fsems.at[1, buf])

    def y_copy(l, buf, action):
        @pl.when(in_ctx)
        def _():
            action(ctx_copy(l, buf))

        @pl.when(jnp.logical_not(in_ctx))
        def _():
            action(lat_copy(l, buf))

    base_copy(0, 0).start()
    for l in range(TILES_PER_GROUP):
        buf = l % 2
        if l + 1 < TILES_PER_GROUP:
            base_copy(l + 1, 1 - buf).start()
        base_copy(l, buf).wait()
        if l >= 2:
            y_copy(l - 2, buf, lambda c: c.wait())
        routed = jnp.concatenate(
            [acc_vmem[pl.ds(l * ROW_TILE * SUBLANES + c, ROW_TILE, stride=LANE_CHUNKS), :]
             for c in range(LANE_CHUNKS)], axis=1)
        gate = gate_ref[pl.ds(_cond_of_tile(tile0 + l), 1), :]
        ybuf_ref[buf] = bbuf_ref[buf] + gate * routed
        y_copy(l, buf, lambda c: c.start())
    for l in range(TILES_PER_GROUP - 2, TILES_PER_GROUP):
        y_copy(l, l % 2, lambda c: c.wait())


N_WEIGHT_BUFS = 2


def _expert_call(blk_exp, n_used, next_exp, slots, ws, x_tiles, wg, wu, wd, base, gate, layer):
    any_spec = pl.BlockSpec(memory_space=pl.ANY)
    return pl.pallas_call(
        functools.partial(_expert_kernel, layer=layer),
        out_shape=STREAM_SHAPES,
        grid_spec=pltpu.PrefetchScalarGridSpec(
            num_scalar_prefetch=3,
            grid=(N_GROUPS,),
            in_specs=[any_spec] * 7 + [pl.BlockSpec((COND_ROWS, D), lambda g, *_: (0, 0))],
            out_specs=(any_spec, any_spec),
            scratch_shapes=[
                pltpu.VMEM((ACC_TILE_ROWS, LANES), F32),
                pltpu.VMEM((ACC_TILE_ROWS, LANES), F32),
                pltpu.VMEM((MOE_BM * SUBLANES, LANES), F32),
                pltpu.VMEM((MOE_BM * SUBLANES, LANES), F32),
                pltpu.VMEM((MOE_BM * SUBLANES, LANES), F32),
                pltpu.VMEM((MOE_BM * SUBLANES, LANES), F32),
                pltpu.VMEM((N_WEIGHT_BUFS, D, D_EXPERT), F32),
                pltpu.VMEM((N_WEIGHT_BUFS, D, D_EXPERT), F32),
                pltpu.VMEM((N_WEIGHT_BUFS, D_EXPERT, D), F32),
                pltpu.VMEM((D, D_EXPERT), BF16),
                pltpu.VMEM((D, D_EXPERT), BF16),
                pltpu.VMEM((D_EXPERT, D), BF16),
                pltpu.VMEM((2, ROW_TILE, D), F32),
                pltpu.VMEM((2, ROW_TILE, D), F32),
                pltpu.SMEM((SLOT_ROWS, LANES), I32),
                pltpu.SMEM((N_ASSIGN,), F32),
                pltpu.SMEM((1,), I32),
                pltpu.SemaphoreType.DMA((3,)),
                pltpu.SemaphoreType.DMA((N_WEIGHT_BUFS, 3)),
                pltpu.SemaphoreType.DMA((2, 2)),
            ]),
        compiler_params=_params(("arbitrary",)),
        name="moe_experts",
    )(blk_exp, n_used, next_exp, slots, ws, x_tiles, wg, wu, wd, base, gate)


BM_SHIFT = MOE_BM.bit_length() - 1


def _plan_kernel(cnt_ref, start_ref, n_used_ref, blk_exp_ref, next_exp_ref):
    for g in range(N_GROUPS):
        def expert(e, carry):
            run, last = carry
            blocks = (cnt_ref[g * N_EXPERTS + e] + MOE_BM - 1) >> BM_SHIFT
            start_ref[g * N_EXPERTS + e] = run << BM_SHIFT

            def mark(b, c):
                blk_exp_ref[g * MOE_NB + b] = e
                return c

            lax.fori_loop(run, run + blocks, mark, 0)
            return run + blocks, jnp.where(blocks > 0, e, last)

        used, last = lax.fori_loop(0, N_EXPERTS, expert, (0, 0))
        n_used_ref[g] = used

        def mark_unused(b, c):
            blk_exp_ref[g * MOE_NB + b] = last
            return c

        lax.fori_loop(used, MOE_NB, mark_unused, 0)

        def link(i, nxt):
            e = N_EXPERTS - 1 - i
            next_exp_ref[g * N_EXPERTS + e] = nxt
            return jnp.where(cnt_ref[g * N_EXPERTS + e] > 0, e, nxt)

        lax.fori_loop(0, N_EXPERTS, link, -1)


def _plan_call(counts):
    smem = pl.BlockSpec(memory_space=pltpu.SMEM)
    return pl.pallas_call(
        _plan_kernel,
        out_shape=(jax.ShapeDtypeStruct((N_GROUPS * N_EXPERTS,), I32),
                   jax.ShapeDtypeStruct((N_GROUPS,), I32),
                   jax.ShapeDtypeStruct((N_GROUPS * MOE_NB,), I32),
                   jax.ShapeDtypeStruct((N_GROUPS * N_EXPERTS,), I32)),
        in_specs=[smem],
        out_specs=(smem, smem, smem, smem),
        name="moe_plan",
    )(counts)


def _moe_layer(y, ada, norm_ffn, router_w, router_bias, wg, wu, wd, sg, su, sd, layer):
    x_tiles, base, keys, ws, cnts = _router_call(
        y, ada, norm_ffn, router_w, router_bias, sg, su, sd, layer)
    starts, n_used, blk_exp, next_exp = _plan_call(cnts[:, :, 0].astype(I32).reshape(-1))
    pos = _pos_call(starts, keys)
    slots = jnp.stack([
        _slot_call(pos[:, g * MOE_GROUP:(g + 1) * MOE_GROUP].reshape(-1))
        for g in range(N_GROUPS)]).reshape(N_GROUPS, SLOT_ROWS, LANES)
    ws_g = ws.reshape(TOP_K, N_GROUPS, MOE_GROUP).transpose(1, 0, 2).reshape(N_GROUPS, N_ASSIGN)
    gate = ada[layer, :, N_ADA - 1, :]
    return _expert_call(blk_exp, n_used, next_exp, slots, ws_g, x_tiles, wg, wu, wd, base, gate, layer)


def kernel(x_prompt, x_sample, cache_k, cache_v, c, c_ctx, ada_w, ada_b, norm_mix, norm_ffn,
           conv_w1, conv_b1, conv_dw, conv_dw_b, conv_norm, conv_w2, conv_b2,
           attn_wqkv, attn_q_norm, attn_k_norm, attn_sink, attn_wo,
           router_w, router_bias, exp_w_gate, exp_w_up, exp_w_down,
           sh_w_gate, sh_w_up, sh_w_down):
    y = (x_prompt.reshape(TP, D), x_sample.reshape(TS, D))
    cond = jnp.concatenate(
        [c_ctx[None, :], c, jnp.zeros((COND_ROWS - N_COND, D), F32)], axis=0)
    ada = _ada_call(cond, ada_w, ada_b).reshape(DEPTH, COND_ROWS, N_ADA, D)

    new_k = new_v = None
    for layer in range(DEPTH):
        j = layer // 2
        if layer % 2 == 0:
            u = _conv_in_call(y, ada, norm_mix, conv_w1, conv_b1, layer, j)
            y = _conv_out_call(u, y, ada, conv_dw, conv_dw_b, conv_norm, conv_w2, conv_b2, layer, j)
        else:
            q, k, v, kt, vt = _qkv_call(y, ada, norm_mix, attn_wqkv, attn_q_norm, attn_k_norm, layer, j)
            cache_shape = (N_CTX_SEQ, 1, N_KV, HEAD_DIM, CTX_LEN)
            new_k = kt.reshape(cache_shape).transpose(0, 1, 4, 2, 3)
            new_v = vt.reshape(cache_shape).transpose(0, 1, 4, 2, 3)
            y = (_attn_ctx_call(q, k, v, attn_sink, attn_wo, y[0], ada, layer, j),
                 _attn_lat_call(q, k, v, cache_k, cache_v, attn_sink, attn_wo, y[1], ada, layer, j))
        y = _moe_layer(y, ada, norm_ffn, router_w, router_bias, exp_w_gate, exp_w_up, exp_w_down,
                       sh_w_gate, sh_w_up, sh_w_down, layer)

    y_p = y[0].reshape(N_CTX_SEQ, CTX_LEN, D)
    y_s = y[1].reshape(N_LAT_SEQ, LAT_LEN, D)
    return (y_p, y_s, new_k, new_v)
```

```python
import functools

import jax
import jax.numpy as jnp
import numpy as np
from jax import lax
from jax.experimental import pallas as pl
from jax.experimental.pallas import tpu as pltpu

F32 = jnp.float32
BF16 = jnp.bfloat16
I32 = jnp.int32

D = 1024
N_CTX_SEQ = 32
CTX_LEN = 256
N_LAT_SEQ = 4
LAT_LEN = 1024
TP = N_CTX_SEQ * CTX_LEN
TS = N_LAT_SEQ * LAT_LEN
T = TP + TS
DEPTH = 2
N_ADA = 6
N_COND = 1 + N_LAT_SEQ
COND_ROWS = 8
CONV_WIDTH = 31
CONV_PAD = CONV_WIDTH // 2
N_HEADS = 16
N_KV = 4
HEAD_DIM = 64
GQA = N_HEADS // N_KV
KV_WIDTH = N_KV * HEAD_DIM
QKV_WIDTH = D + 2 * KV_WIDTH
WINDOW = 128
GRID_W = 64
ROPE_PAIRS = HEAD_DIM // 4
ROPE_THETA = 10000.0
N_EXPERTS = 64
N_EGROUPS = 8
EGROUP = N_EXPERTS // N_EGROUPS
TOPK_GROUPS = 4
TOP_K = 8
D_EXPERT = 256
D_SHARED = 256
ROUTED_SCALE = 2.5
NORM_EPS = 1e-6

LANES = 128
SUBLANES = 8
VMEM_LIMIT = 56 * 1024 * 1024

ROW_TILE = 256
N_TILES = T // ROW_TILE
N_CTX_TILES = TP // ROW_TILE
LAT_TILES_PER_SEQ = LAT_LEN // ROW_TILE
HALO = 16
LANE_CHUNKS = D // LANES

MOE_GROUP = 4096
N_GROUPS = T // MOE_GROUP
N_CTX_GROUPS = TP // MOE_GROUP
TILES_PER_GROUP = MOE_GROUP // ROW_TILE
MOE_BM = 256
MOE_NB = MOE_GROUP * TOP_K // MOE_BM + N_EXPERTS
N_ASSIGN = MOE_GROUP * TOP_K
ACC_ROWS = MOE_GROUP + SUBLANES
RANK_BITS = 12
Q_BLOCK = 128


def _cond_of_tile(i):
    return jnp.where(i < N_CTX_TILES, 0, 1 + (i - N_CTX_TILES) // LAT_TILES_PER_SEQ)


def _params(sem, vmem=VMEM_LIMIT):
    return pltpu.CompilerParams(dimension_semantics=sem, vmem_limit_bytes=vmem)


def _split(a):
    hi = a.astype(BF16)
    lo = (a - hi.astype(F32)).astype(BF16)
    return hi, lo


def _dot3(a, b):
    a_hi, a_lo = _split(a)
    b_hi, b_lo = _split(b)
    d = functools.partial(jnp.dot, preferred_element_type=F32)
    return d(a_hi, b_hi) + d(a_lo, b_hi) + d(a_hi, b_lo)


def _dot2(a, b_bf16):
    a_hi, a_lo = _split(a)
    d = functools.partial(jnp.dot, preferred_element_type=F32)
    return d(a_hi, b_bf16) + d(a_lo, b_bf16)


def _rms(x, g):
    return x * lax.rsqrt(jnp.mean(x * x, axis=-1, keepdims=True) + NORM_EPS) * g


def _modulate(x, g, shift, scale):
    return _rms(x, g) * (1.0 + scale) + shift


def _silu(x):
    return x * jax.nn.sigmoid(x)


def _ada_spec(layer):
    return pl.BlockSpec((None, None, N_ADA, D), lambda i: (layer, _cond_of_tile(i), 0, 0))


def _tile_spec(width=D):
    return pl.BlockSpec((ROW_TILE, width), lambda i: (i, 0))


STREAM_SPECS = [
    pl.BlockSpec((ROW_TILE, D), lambda i: (jnp.minimum(i, N_CTX_TILES - 1), 0)),
    pl.BlockSpec((ROW_TILE, D), lambda i: (jnp.maximum(i - N_CTX_TILES, 0), 0)),
]
STREAM_SHAPES = (jax.ShapeDtypeStruct((TP, D), F32), jax.ShapeDtypeStruct((TS, D), F32))


def _read_stream(ctx_ref, lat_ref):
    return jnp.where(pl.program_id(0) < N_CTX_TILES, ctx_ref[...], lat_ref[...])


def _write_stream(ctx_ref, lat_ref, value):
    i = pl.program_id(0)

    @pl.when(i < N_CTX_TILES)
    def _():
        ctx_ref[...] = value

    @pl.when(i >= N_CTX_TILES)
    def _():
        lat_ref[...] = value


def _const_spec(shape):
    nd = len(shape)
    return pl.BlockSpec(shape, lambda *_: (0,) * nd)


def _layer_spec(layer, shape):
    nd = len(shape)
    return pl.BlockSpec((None,) + tuple(shape), lambda *_: (layer,) + (0,) * nd)


ADA_NB = 1536


def _ada_kernel(c_ref, w_ref, b_ref, o_ref):
    o_ref[...] = _dot3(_silu(c_ref[...]), w_ref[...]) + b_ref[...]


def _ada_call(cond, ada_w, ada_b):
    return pl.pallas_call(
        _ada_kernel,
        out_shape=jax.ShapeDtypeStruct((DEPTH, COND_ROWS, N_ADA * D), F32),
        grid=(DEPTH, N_ADA * D // ADA_NB),
        in_specs=[
            pl.BlockSpec((COND_ROWS, D), lambda l, n: (0, 0)),
            pl.BlockSpec((None, D, ADA_NB), lambda l, n: (l, 0, n)),
            pl.BlockSpec((None, 1, ADA_NB), lambda l, n: (l, 0, n)),
        ],
        out_specs=pl.BlockSpec((None, COND_ROWS, ADA_NB), lambda l, n: (l, 0, n)),
        compiler_params=_params(("arbitrary", "arbitrary")),
        name="ada_params",
    )(cond, ada_w, ada_b.reshape(DEPTH, 1, N_ADA * D))


def _conv_in_kernel(xc_ref, xl_ref, ada_ref, g_ref, w1_ref, b1_ref, u_ref, w1b_ref):
    @pl.when(pl.program_id(0) == 0)
    def _():
        w1b_ref[...] = w1_ref[...].astype(BF16)

    h = _modulate(_read_stream(xc_ref, xl_ref), g_ref[...], ada_ref[0:1, :], ada_ref[1:2, :])
    u = jnp.dot(h.astype(BF16), w1b_ref[...], preferred_element_type=F32) + b1_ref[...]
    u_ref[...] = u[:, :D] * jax.nn.sigmoid(u[:, D:])


def _conv_in_call(y, ada, norm_mix, conv_w1, conv_b1, layer, j):
    return pl.pallas_call(
        _conv_in_kernel,
        out_shape=jax.ShapeDtypeStruct((T, D), F32),
        grid=(N_TILES,),
        in_specs=STREAM_SPECS + [
            _ada_spec(layer),
            _layer_spec(layer, (1, D)),
            _layer_spec(j, (D, 2 * D)),
            _layer_spec(j, (1, 2 * D)),
        ],
        out_specs=_tile_spec(),
        scratch_shapes=[pltpu.VMEM((D, 2 * D), BF16)],
        compiler_params=_params(("arbitrary",)),
        name="conv_in",
    )(*y, ada, norm_mix.reshape(DEPTH, 1, D), conv_w1, conv_b1.reshape(-1, 1, 2 * D))


CONV_ROWS = 64
BUF_ROWS = ROW_TILE + 2 * HALO


def _conv_out_kernel(u_ref, up_ref, un_ref, yc_ref, yl_ref, ada_ref, dw_ref, dwb_ref, gn_ref,
                     w2_ref, b2_ref, oc_ref, ol_ref, buf_ref, z_ref, w2b_ref):
    i = pl.program_id(0)

    @pl.when(i == 0)
    def _():
        w2b_ref[...] = w2_ref[...].astype(BF16)

    lat = i >= N_CTX_TILES
    pos = jnp.where(lat, i - N_CTX_TILES, 0) % LAT_TILES_PER_SEQ
    has_prev = lat & (pos != 0)
    has_next = lat & (pos != LAT_TILES_PER_SEQ - 1)
    for c in range(LANE_CHUNKS):
        cs = slice(c * LANES, (c + 1) * LANES)
        buf_ref[c, 0:HALO, :] = jnp.where(has_prev, up_ref[:, cs], 0.0)
        buf_ref[c, HALO:HALO + ROW_TILE, :] = u_ref[:, cs]
        buf_ref[c, HALO + ROW_TILE:BUF_ROWS, :] = jnp.where(has_next, un_ref[:, cs], 0.0)

    off = HALO - CONV_PAD
    for c in range(LANE_CHUNKS):
        cs = slice(c * LANES, (c + 1) * LANES)
        for r0 in range(0, ROW_TILE, CONV_ROWS):
            acc = jnp.broadcast_to(dwb_ref[:, cs], (CONV_ROWS, LANES))
            for k in range(CONV_WIDTH):
                win = buf_ref[c, r0 + k + off:r0 + k + off + CONV_ROWS, :]
                acc = acc + dw_ref[k:k + 1, cs] * win
            z_ref[r0:r0 + CONV_ROWS, cs] = acc

    z = _silu(_rms(z_ref[...], gn_ref[...]))
    m = jnp.dot(z.astype(BF16), w2b_ref[...], preferred_element_type=F32) + b2_ref[...]
    _write_stream(oc_ref, ol_ref, _read_stream(yc_ref, yl_ref) + ada_ref[2:3, :] * m)


def _conv_out_call(u, y, ada, conv_dw, conv_dw_b, conv_norm, conv_w2, conv_b2, layer, j):
    halos_per_tile = ROW_TILE // HALO
    last_halo = T // HALO - 1
    return pl.pallas_call(
        _conv_out_kernel,
        out_shape=STREAM_SHAPES,
        grid=(N_TILES,),
        in_specs=[
            _tile_spec(),
            pl.BlockSpec((HALO, D), lambda i: (jnp.maximum(i * halos_per_tile - 1, 0), 0)),
            pl.BlockSpec((HALO, D), lambda i: (jnp.minimum((i + 1) * halos_per_tile, last_halo), 0)),
        ] + STREAM_SPECS + [
            _ada_spec(layer),
            _layer_spec(j, (CONV_WIDTH + 1, D)),
            _layer_spec(j, (1, D)),
            _layer_spec(j, (1, D)),
            _layer_spec(j, (D, D)),
            _layer_spec(j, (1, D)),
        ],
        out_specs=STREAM_SPECS,
        scratch_shapes=[pltpu.VMEM((LANE_CHUNKS, BUF_ROWS, LANES), F32), pltpu.VMEM((ROW_TILE, D), F32),
                        pltpu.VMEM((D, D), BF16)],
        compiler_params=_params(("arbitrary",)),
        name="conv_out",
    )(u, u, u, *y, ada, jnp.pad(conv_dw, ((0, 0), (0, 1), (0, 0))), conv_dw_b.reshape(-1, 1, D),
      conv_norm.reshape(-1, 1, D), conv_w2, conv_b2.reshape(-1, 1, D))


def _swap_pairs(x, width):
    lane = lax.broadcasted_iota(I32, x.shape, 1)
    first = (lane % (2 * ROPE_PAIRS)) < ROPE_PAIRS
    return jnp.where(first, pltpu.roll(x, width - ROPE_PAIRS, 1), pltpu.roll(x, ROPE_PAIRS, 1))


def _qkv_kernel(xc_ref, xl_ref, ada_ref, g_ref, w_ref, qn_ref, kn_ref, cos_ref, sin_ref,
                q_ref, k_ref, v_ref, kt_ref, vt_ref, wb_ref, hs_ref):
    i = pl.program_id(0)

    @pl.when(i == 0)
    def _():
        wb_ref[...] = w_ref[...].astype(BF16)
        r = lax.broadcasted_iota(I32, (D, D), 0) // HEAD_DIM
        c = lax.broadcasted_iota(I32, (D, D), 1) // HEAD_DIM
        hs_ref[...] = jnp.where(r == c, 1.0 / HEAD_DIM, 0.0).astype(BF16)

    h = _modulate(_read_stream(xc_ref, xl_ref), g_ref[...], ada_ref[0:1, :], ada_ref[1:2, :])
    qkv = jnp.dot(h.astype(BF16), wb_ref[...], preferred_element_type=F32)
    q = qkv[:, :D]
    k = qkv[:, D:D + KV_WIDTH]
    v_ref[...] = qkv[:, D + KV_WIDTH:].astype(BF16)
    q_ms = _dot2(q * q, hs_ref[...])
    k_ms = _dot2(k * k, hs_ref[0:KV_WIDTH, 0:KV_WIDTH])
    qn = q * lax.rsqrt(q_ms + NORM_EPS) * qn_ref[...]
    kn = k * lax.rsqrt(k_ms + NORM_EPS) * kn_ref[...]

    @pl.when(i < N_CTX_TILES)
    def _():
        q_ref[...] = qn.astype(BF16)
        k_ref[...] = kn.astype(BF16)
        kt_ref[...] = kn.T
        vt_ref[...] = qkv[:, D + KV_WIDTH:].T

    @pl.when(i >= N_CTX_TILES)
    def _():
        cos = cos_ref[...]
        sin = sin_ref[...]
        q_ref[...] = (qn * cos + _swap_pairs(qn, D) * sin).astype(BF16)
        k_ref[...] = (kn * cos[:, :KV_WIDTH]
                      + _swap_pairs(kn, KV_WIDTH) * sin[:, :KV_WIDTH]).astype(BF16)


def _rope_tables():
    pos = np.arange(LAT_LEN)
    row = (pos // GRID_W).astype(np.float32)
    col = (pos % GRID_W).astype(np.float32)
    inv_freq = np.float32(ROPE_THETA) ** (-np.arange(ROPE_PAIRS, dtype=np.float32) / ROPE_PAIRS)
    d = np.arange(D) % HEAD_DIM
    freq = inv_freq[d % ROPE_PAIRS].astype(np.float32)
    p = np.where((d >= HEAD_DIM // 2)[None, :], col[:, None], row[:, None])
    ang = (p * freq[None, :]).astype(np.float32)
    first = (d % (2 * ROPE_PAIRS)) < ROPE_PAIRS
    cos = np.cos(ang).astype(np.float32)
    sin = np.sin(ang).astype(np.float32)
    return jnp.asarray(cos), jnp.asarray(np.where(first[None, :], -sin, sin))


def _qkv_call(y, ada, norm_mix, wqkv, q_norm, k_norm, layer, j):
    cos, sin = _rope_tables()
    table_spec = pl.BlockSpec(
        (ROW_TILE, D), lambda i: (jnp.maximum(i - N_CTX_TILES, 0) % LAT_TILES_PER_SEQ, 0))
    cache_spec = pl.BlockSpec((None, KV_WIDTH, CTX_LEN), lambda i: (jnp.minimum(i, N_CTX_SEQ - 1), 0, 0))
    return pl.pallas_call(
        _qkv_kernel,
        out_shape=(jax.ShapeDtypeStruct((T, D), BF16), jax.ShapeDtypeStruct((T, KV_WIDTH), BF16),
                   jax.ShapeDtypeStruct((T, KV_WIDTH), BF16),
                   jax.ShapeDtypeStruct((N_CTX_SEQ, KV_WIDTH, CTX_LEN), F32),
                   jax.ShapeDtypeStruct((N_CTX_SEQ, KV_WIDTH, CTX_LEN), F32)),
        grid=(N_TILES,),
        in_specs=STREAM_SPECS + [
            _ada_spec(layer),
            _layer_spec(layer, (1, D)),
            _layer_spec(j, (D, QKV_WIDTH)),
            _const_spec((1, D)),
            _const_spec((1, KV_WIDTH)),
            table_spec,
            table_spec,
        ],
        out_specs=(_tile_spec(), _tile_spec(KV_WIDTH), _tile_spec(KV_WIDTH), cache_spec, cache_spec),
        scratch_shapes=[pltpu.VMEM((D, QKV_WIDTH), BF16), pltpu.VMEM((D, D), BF16)],
        compiler_params=_params(("arbitrary",)),
        name="qkv",
    )(*y, ada, norm_mix.reshape(DEPTH, 1, D), wqkv,
      jnp.tile(q_norm[j], N_HEADS).reshape(1, D), jnp.tile(k_norm[j], N_KV).reshape(1, KV_WIDTH),
      cos, sin)


HEADS_PER_TILE = LANES // HEAD_DIM
TILES_PER_KV = GQA // HEADS_PER_TILE


def _gqa_attention(q, segments, sink_ref, rows):
    d = functools.partial(jnp.dot, preferred_element_type=F32)
    half = lax.broadcasted_iota(I32, (rows, LANES), 1) < HEAD_DIM
    q_zero = jnp.zeros((rows, LANES), BF16)

    scores, sinks, kvs = [], [], []
    for kv in range(N_KV):
        sl = slice(kv * HEAD_DIM, (kv + 1) * HEAD_DIM)
        parts = []
        for t in range(TILES_PER_KV):
            tile = kv * TILES_PER_KV + t
            qt = q[:, tile * LANES:(tile + 1) * LANES]
            parts += [jnp.where(half, qt, q_zero), jnp.where(half, q_zero, qt)]
        qg = jnp.concatenate(parts, axis=0)
        seg_scores, seg_kv = [], []
        for k, v, bias in segments:
            kk = jnp.concatenate([k[:, sl], k[:, sl]], axis=1)
            s = lax.dot_general(qg, kk, (((1,), (1,)), ((), ())),
                                preferred_element_type=F32) * (HEAD_DIM ** -0.5)
            seg_scores.append(s if bias is None else s + bias)
            v_zero = jnp.zeros_like(v[:, sl])
            seg_kv.append((jnp.concatenate([v[:, sl], v_zero], axis=1),
                           jnp.concatenate([v_zero, v[:, sl]], axis=1)))
        scores.append(seg_scores)
        kvs.append(seg_kv)
        sinks.append(jnp.concatenate(
            [jnp.broadcast_to(sink_ref[0:1, kv * GQA + g:kv * GQA + g + 1], (rows, 1))
             for g in range(GQA)], axis=0))

    maxes = []
    for kv in range(N_KV):
        m = sinks[kv]
        for s in scores[kv]:
            m = jnp.maximum(m, jnp.max(s, axis=1, keepdims=True))
        maxes.append(m)
    probs = [[jnp.exp(s - maxes[kv]).astype(BF16) for s in scores[kv]] for kv in range(N_KV)]

    tiles = []
    for kv in range(N_KV):
        den = jnp.exp(sinks[kv] - maxes[kv])
        for p in probs[kv]:
            den = den + d(p, jnp.ones((p.shape[1], LANES), BF16))
        inv = 1.0 / den
        for t in range(TILES_PER_KV):
            tile = None
            for h in range(HEADS_PER_TILE):
                g = t * HEADS_PER_TILE + h
                rs = slice(g * rows, (g + 1) * rows)
                o = None
                for p, v_halves in zip(probs[kv], kvs[kv]):
                    part = d(p[rs, :], v_halves[h])
                    o = part if o is None else o + part
                o = o * inv[rs, :]
                tile = o if tile is None else tile + o
            tiles.append(tile)
    return jnp.concatenate(tiles, axis=1)


def _attn_ctx_kernel(q_ref, k_ref, v_ref, sink_ref, wo_ref, y_ref, ada_ref, o_ref, wob_ref):
    @pl.when(pl.program_id(0) == 0)
    def _():
        wob_ref[...] = wo_ref[...].astype(BF16)

    segments = [(k_ref[...].astype(BF16), v_ref[...].astype(BF16), None)]
    o = _gqa_attention(q_ref[...].astype(BF16), segments, sink_ref, CTX_LEN)
    m_out = jnp.dot(o.astype(BF16), wob_ref[...], preferred_element_type=F32)
    o_ref[...] = y_ref[...] + ada_ref[2:3, :] * m_out


def _attn_ctx_call(q, k, v, sink, wo, y_ctx, ada, layer, j):
    return pl.pallas_call(
        _attn_ctx_kernel,
        out_shape=STREAM_SHAPES[0],
        grid=(N_CTX_SEQ,),
        in_specs=[
            _tile_spec(), _tile_spec(KV_WIDTH), _tile_spec(KV_WIDTH),
            _layer_spec(j, (1, N_HEADS)),
            _layer_spec(j, (D, D)),
            _tile_spec(),
            _ada_spec(layer),
        ],
        out_specs=_tile_spec(),
        scratch_shapes=[pltpu.VMEM((D, D), BF16)],
        compiler_params=_params(("arbitrary",)),
        name="attn_ctx",
    )(q, k, v, sink.reshape(-1, 1, N_HEADS), wo, y_ctx, ada)


LAT_WIN = 3 * WINDOW
N_QBLOCKS = LAT_LEN // Q_BLOCK


def _attn_lat_kernel(q_ref, k_ref, v_ref, kc_ref, vc_ref, sink_ref, wo_ref, y_ref, ada_ref,
                     o_ref, wob_ref):
    b = pl.program_id(0)
    n = pl.program_id(1)

    @pl.when((b == 0) & (n == 0))
    def _():
        wob_ref[...] = wo_ref[...].astype(BF16)

    start = pl.multiple_of(jnp.clip((n - 1) * WINDOW, 0, LAT_LEN - LAT_WIN), WINDOW)
    kw = k_ref[pl.ds(start, LAT_WIN), :].astype(BF16)
    vw = v_ref[pl.ds(start, LAT_WIN), :].astype(BF16)
    qpos = n * Q_BLOCK + lax.broadcasted_iota(I32, (GQA * Q_BLOCK, LAT_WIN), 0) % Q_BLOCK
    kpos = start + lax.broadcasted_iota(I32, (GQA * Q_BLOCK, LAT_WIN), 1)
    band = jnp.where(jnp.abs(kpos - qpos) <= WINDOW, 0.0, -jnp.inf)
    segments = [(kw, vw, band),
                (kc_ref[...].astype(BF16), vc_ref[...].astype(BF16), None)]
    o = _gqa_attention(q_ref[...].astype(BF16), segments, sink_ref, Q_BLOCK)
    m_out = jnp.dot(o.astype(BF16), wob_ref[...], preferred_element_type=F32)
    o_ref[...] = y_ref[...] + ada_ref[2:3, :] * m_out


def _attn_lat_call(q, k, v, cache_k, cache_v, sink, wo, y_lat, ada, layer, j):
    q_row0 = TP // Q_BLOCK
    seq0 = TP // LAT_LEN
    qspec = pl.BlockSpec((Q_BLOCK, D), lambda b, n: (q_row0 + b * N_QBLOCKS + n, 0))
    yspec = pl.BlockSpec((Q_BLOCK, D), lambda b, n: (b * N_QBLOCKS + n, 0))
    kvspec = pl.BlockSpec((LAT_LEN, KV_WIDTH), lambda b, n: (seq0 + b, 0))
    cspec = pl.BlockSpec((None, None, CTX_LEN, KV_WIDTH), lambda b, n: (b, j, 0, 0))
    ck = cache_k.reshape(N_LAT_SEQ, -1, CTX_LEN, KV_WIDTH)
    cv = cache_v.reshape(N_LAT_SEQ, -1, CTX_LEN, KV_WIDTH)
    return pl.pallas_call(
        _attn_lat_kernel,
        out_shape=STREAM_SHAPES[1],
        grid=(N_LAT_SEQ, N_QBLOCKS),
        in_specs=[
            qspec, kvspec, kvspec, cspec, cspec,
            _layer_spec(j, (1, N_HEADS)),
            _layer_spec(j, (D, D)),
            yspec,
            pl.BlockSpec((None, None, N_ADA, D), lambda b, n: (layer, 1 + b, 0, 0)),
        ],
        out_specs=yspec,
        scratch_shapes=[pltpu.VMEM((D, D), BF16)],
        compiler_params=_params(("arbitrary", "arbitrary")),
        name="attn_lat",
    )(q, k, v, ck, cv, sink.reshape(-1, 1, N_HEADS), wo, y_lat, ada)


def _router_kernel(yc_ref, yl_ref, ada_ref, g_ref, rw_ref, rb_ref, sg_ref, su_ref, sd_ref,
                   h_ref, base_ref, key_ref, w_ref, cnt_ref,
                   carry_ref, tri_ref, sgb_ref, sub_ref, sdb_ref):
    i = pl.program_id(0)

    @pl.when(i == 0)
    def _():
        sgb_ref[...] = sg_ref[...].astype(BF16)
        sub_ref[...] = su_ref[...].astype(BF16)
        sdb_ref[...] = sd_ref[...].astype(BF16)
        r = lax.broadcasted_iota(I32, (ROW_TILE, ROW_TILE), 0)
        c = lax.broadcasted_iota(I32, (ROW_TILE, ROW_TILE), 1)
        tri_ref[...] = jnp.where(r < c, 1.0, 0.0).astype(BF16)

    @pl.when(i % TILES_PER_GROUP == 0)
    def _():
        carry_ref[...] = jnp.zeros_like(carry_ref)

    y = _read_stream(yc_ref, yl_ref)
    h = _modulate(y, g_ref[...], ada_ref[3:4, :], ada_ref[4:5, :])
    for c in range(LANE_CHUNKS):
        h_ref[pl.ds(c, ROW_TILE, stride=LANE_CHUNKS), :] = h[:, c * LANES:(c + 1) * LANES]
    hb = h.astype(BF16)

    d = functools.partial(jnp.dot, preferred_element_type=F32)
    act = (_silu(d(hb, sgb_ref[...])) * d(hb, sub_ref[...])).astype(BF16)
    base_ref[...] = y + ada_ref[5:6, :] * d(act, sdb_ref[...])

    logits = _dot3(h, rw_ref[...])
    logits_t = jnp.concatenate([logits, jnp.zeros_like(logits)], axis=1).T[:N_EXPERTS]
    scores = jax.nn.sigmoid(logits_t)
    shape3 = (N_EGROUPS, EGROUP, ROW_TILE)
    scores3 = scores.reshape(shape3)
    choice3 = (scores + rb_ref[...]).reshape(shape3)
    sub = lax.broadcasted_iota(I32, shape3, 1)
    eidx = lax.broadcasted_iota(I32, shape3, 0) * EGROUP + sub
    gidx = lax.broadcasted_iota(I32, (N_EGROUPS, 1, ROW_TILE), 0)
    neg = -jnp.inf

    m1 = jnp.max(choice3, axis=1, keepdims=True)
    i1 = jnp.min(jnp.where(choice3 == m1, sub, EGROUP), axis=1, keepdims=True)
    m2 = jnp.max(jnp.where(sub == i1, neg, choice3), axis=1, keepdims=True)
    gscore = m1 + m2
    allowed = jnp.zeros((N_EGROUPS, 1, ROW_TILE), jnp.bool_)
    for _ in range(TOPK_GROUPS):
        gm = jnp.max(gscore, axis=0, keepdims=True)
        gi = jnp.min(jnp.where(gscore == gm, gidx, N_EGROUPS), axis=0, keepdims=True)
        hit = gidx == gi
        allowed = allowed | hit
        gscore = jnp.where(hit, neg, gscore)

    def reduce_experts(fn, x):
        return fn(fn(x, axis=0, keepdims=True), axis=1, keepdims=True)

    cm = jnp.where(allowed, choice3, neg)
    picked = jnp.zeros(shape3, F32)
    e_rows, s_rows = [], []
    for _ in range(TOP_K):
        m = reduce_experts(jnp.max, cm)
        ik = reduce_experts(jnp.min, jnp.where(cm == m, eidx, N_EXPERTS))
        hit = eidx == ik
        e_rows.append(ik)
        s_rows.append(reduce_experts(jnp.sum, jnp.where(hit, scores3, 0.0)))
        cm = jnp.where(hit, neg, cm)
        picked = jnp.where(hit, 1.0, picked)

    picked2 = picked.reshape(N_EXPERTS, ROW_TILE)
    cum3 = (d(picked2.astype(BF16), tri_ref[...]) + carry_ref[...]).reshape(shape3)
    carry_ref[...] = carry_ref[...] + jnp.sum(picked2, axis=1, keepdims=True)
    cnt_ref[...] = jnp.broadcast_to(carry_ref[...], (N_EXPERTS, LANES))

    denom = functools.reduce(lambda a, b: a + b, s_rows)
    key_rows, w_rows = [], []
    for kk in range(TOP_K):
        rank = reduce_experts(jnp.sum, jnp.where(eidx == e_rows[kk], cum3, 0.0)).astype(I32)
        key_rows.append(((e_rows[kk] << RANK_BITS) | rank).reshape(1, ROW_TILE))
        w_rows.append((s_rows[kk] / denom * ROUTED_SCALE).reshape(1, ROW_TILE))
    key_ref[...] = jnp.concatenate(key_rows, axis=0)
    w_ref[...] = jnp.concatenate(w_rows, axis=0)


def _router_call(y, ada, norm_ffn, router_w, router_bias, sg, su, sd, layer):
    return pl.pallas_call(
        _router_kernel,
        out_shape=(
            jax.ShapeDtypeStruct((N_GROUPS, MOE_GROUP * LANE_CHUNKS, LANES), F32),
            jax.ShapeDtypeStruct((T, D), F32),
            jax.ShapeDtypeStruct((TOP_K, T), I32),
            jax.ShapeDtypeStruct((TOP_K, T), F32),
            jax.ShapeDtypeStruct((N_GROUPS, N_EXPERTS, LANES), F32),
        ),
        grid=(N_TILES,),
        in_specs=STREAM_SPECS + [
            _ada_spec(layer),
            _layer_spec(layer, (1, D)),
            _layer_spec(layer, (D, N_EXPERTS)),
            _layer_spec(layer, (N_EXPERTS, 1)),
            _layer_spec(layer, (D, D_SHARED)),
            _layer_spec(layer, (D, D_SHARED)),
            _layer_spec(layer, (D_SHARED, D)),
        ],
        out_specs=(
            pl.BlockSpec((None, ROW_TILE * LANE_CHUNKS, LANES),
                         lambda i: (i // TILES_PER_GROUP, i % TILES_PER_GROUP, 0)),
            _tile_spec(),
            pl.BlockSpec((TOP_K, ROW_TILE), lambda i: (0, i)),
            pl.BlockSpec((TOP_K, ROW_TILE), lambda i: (0, i)),
            pl.BlockSpec((None, N_EXPERTS, LANES), lambda i: (i // TILES_PER_GROUP, 0, 0)),
        ),
        scratch_shapes=[
            pltpu.VMEM((N_EXPERTS, 1), F32),
            pltpu.VMEM((ROW_TILE, ROW_TILE), BF16),
            pltpu.VMEM((D, D_SHARED), BF16),
            pltpu.VMEM((D, D_SHARED), BF16),
            pltpu.VMEM((D_SHARED, D), BF16),
        ],
        compiler_params=_params(("arbitrary",)),
        name="router",
    )(*y, ada, norm_ffn.reshape(DEPTH, 1, D), router_w, router_bias.reshape(DEPTH, N_EXPERTS, 1),
      sg, su, sd)


SLOT_WORDS = (MOE_NB + 3) * MOE_BM
SLOT_ROWS = SLOT_WORDS // LANES


def _pos_kernel(start_ref, key_ref, pos_ref):
    g = pl.program_id(0)
    key = key_ref[...]
    e = key >> RANK_BITS
    pos = key & ((1 << RANK_BITS) - 1)
    for ee in range(N_EXPERTS):
        pos = pos + jnp.where(e == ee, start_ref[g * N_EXPERTS + ee], 0)
    pos_ref[...] = pos


def _pos_call(starts, keys):
    spec = pl.BlockSpec((TOP_K, MOE_GROUP), lambda g, s: (0, g))
    return pl.pallas_call(
        _pos_kernel,
        out_shape=jax.ShapeDtypeStruct((TOP_K, T), I32),
        grid_spec=pltpu.PrefetchScalarGridSpec(
            num_scalar_prefetch=1, grid=(N_GROUPS,), in_specs=[spec], out_specs=spec),
        compiler_params=_params(("arbitrary",)),
        name="moe_pos",
    )(starts, keys)


SLOT_ROW_BITS = 16
DUMMY_SLOT = MOE_GROUP * SUBLANES
SLOT_UNROLL = 4


def _slot_kernel(pos_ref, init_hbm, slot_ref, sem):
    fill = pltpu.make_async_copy(init_hbm, slot_ref, sem)
    fill.start()
    fill.wait()
    per_token = SUBLANES + (1 << SLOT_ROW_BITS)

    def place(t, carry):
        word = t * per_token
        for k in range(TOP_K):
            slot_ref[pos_ref[k * MOE_GROUP + t]] = word + ((k * MOE_GROUP) << SLOT_ROW_BITS)
        return carry

    lax.fori_loop(0, MOE_GROUP, place, 0, unroll=SLOT_UNROLL)


def _slot_call(pos):
    smem = pl.BlockSpec(memory_space=pltpu.SMEM)
    return pl.pallas_call(
        _slot_kernel,
        out_shape=jax.ShapeDtypeStruct((SLOT_WORDS,), I32),
        in_specs=[smem, pl.BlockSpec(memory_space=pl.ANY)],
        out_specs=smem,
        scratch_shapes=[pltpu.SemaphoreType.DMA],
        name="moe_slots",
    )(pos, jnp.full((SLOT_WORDS,), DUMMY_SLOT, I32))


TILE_ROWS = MOE_GROUP * SUBLANES
ACC_TILE_ROWS = ACC_ROWS * SUBLANES
RMW_ROWS = 16


def _expert_kernel(blk_exp_ref, n_used_ref, next_exp_ref,
                   slot_hbm, w_hbm, x_hbm, wg_hbm, wu_hbm, wd_hbm, base_hbm, gate_ref,
                   yc_hbm, yl_hbm,
                   x_vmem, acc_vmem, gat0_ref, gat1_ref, yt0_ref, yt1_ref,
                   wgf_ref, wuf_ref, wdf_ref, wgb_ref, wub_ref, wdb_ref, bbuf_ref, ybuf_ref,
                   slot_smem, w_smem, turn_ref, sems, wsems, fsems, *, layer):
    g = pl.program_id(0)
    rows_per_block = MOE_BM // LANES

    def slot_word(block, r):
        return slot_smem[block * rows_per_block + r // LANES, r % LANES]

    def tile_rows(word):
        return pl.ds(pl.multiple_of(word & ((1 << SLOT_ROW_BITS) - 1), SUBLANES), SUBLANES)

    def gather(block, gat_ref):
        for r in range(MOE_BM):
            gat_ref[pl.ds(r * SUBLANES, SUBLANES), :] = x_vmem[tile_rows(slot_word(block, r)), :]

    def expert_mlp(gat_ref, yt_ref):
        xs = jnp.concatenate(
            [gat_ref[pl.ds(c, MOE_BM, stride=LANE_CHUNKS), :] for c in range(LANE_CHUNKS)],
            axis=1).astype(BF16)
        d = functools.partial(jnp.dot, preferred_element_type=F32)
        act = (_silu(d(xs, wgb_ref[...])) * d(xs, wub_ref[...])).astype(BF16)
        yb = d(act, wdb_ref[...])
        for c in range(LANE_CHUNKS):
            yt_ref[pl.ds(c, MOE_BM, stride=LANE_CHUNKS), :] = yb[:, c * LANES:(c + 1) * LANES]

    def combine(block, yt_ref):
        for r0 in range(0, MOE_BM, RMW_ROWS):
            words = [slot_word(block, r) for r in range(r0, r0 + RMW_ROWS)]
            new = [acc_vmem[tile_rows(word), :]
                   + w_smem[word >> SLOT_ROW_BITS] * yt_ref[pl.ds(r * SUBLANES, SUBLANES), :]
                   for r, word in zip(range(r0, r0 + RMW_ROWS), words)]
            for word, tile in zip(words, new):
                acc_vmem[tile_rows(word), :] = tile

    def group_copies():
        return (pltpu.make_async_copy(x_hbm.at[g], x_vmem.at[pl.ds(0, TILE_ROWS)], sems.at[0]),
                pltpu.make_async_copy(slot_hbm.at[g], slot_smem, sems.at[1]),
                pltpu.make_async_copy(w_hbm.at[g], w_smem, sems.at[2]))

    def weight_copies(e, buf):
        return (pltpu.make_async_copy(wg_hbm.at[layer, e], wgf_ref.at[buf], wsems.at[buf, 0]),
                pltpu.make_async_copy(wu_hbm.at[layer, e], wuf_ref.at[buf], wsems.at[buf, 1]),
                pltpu.make_async_copy(wd_hbm.at[layer, e], wdf_ref.at[buf], wsems.at[buf, 2]))

    for c in group_copies():
        c.start()
    turn_ref[0] = 0
    for c in weight_copies(blk_exp_ref[g * MOE_NB], 0):
        c.start()
    acc_vmem[...] = jnp.zeros_like(acc_vmem)
    x_vmem[pl.ds(TILE_ROWS, ACC_TILE_ROWS - TILE_ROWS), :] = jnp.zeros(
        (ACC_TILE_ROWS - TILE_ROWS, LANES), F32)
    yt1_ref[...] = jnp.zeros_like(yt1_ref)
    for c in group_copies():
        c.wait()
    gather(0, gat0_ref)

    def step(j, gat_cur, gat_next, yt_cur, yt_prev):
        idx = g * MOE_NB + jnp.minimum(j, MOE_NB - 1)
        e_cur = blk_exp_ref[idx]
        e_prev = blk_exp_ref[jnp.maximum(idx - 1, 0)]

        @pl.when((j == 0) | (e_cur != e_prev))
        def _():
            buf = turn_ref[0]
            for c in weight_copies(e_cur, buf):
                c.wait()
            wgb_ref[...] = wgf_ref[buf].astype(BF16)
            wub_ref[...] = wuf_ref[buf].astype(BF16)
            wdb_ref[...] = wdf_ref[buf].astype(BF16)
            e_next = next_exp_ref[g * N_EXPERTS + e_cur]

            @pl.when(e_next >= 0)
            def _():
                for c in weight_copies(e_next, 1 - buf):
                    c.start()

            turn_ref[0] = 1 - buf

        gather(j + 1, gat_next)
        expert_mlp(gat_cur, yt_cur)
        combine(jnp.maximum(j - 1, 0), yt_prev)

    def step_pair(i, carry):
        step(2 * i, gat0_ref, gat1_ref, yt0_ref, yt1_ref)
        step(2 * i + 1, gat1_ref, gat0_ref, yt1_ref, yt0_ref)
        return carry

    lax.fori_loop(0, (n_used_ref[g] + 2) // 2, step_pair, 0)

    tile0 = g * TILES_PER_GROUP
    in_ctx = g < N_CTX_GROUPS

    def base_copy(l, buf):
        return pltpu.make_async_copy(
            base_hbm.at[pl.ds((tile0 + l) * ROW_TILE, ROW_TILE)], bbuf_ref.at[buf], fsems.at[0, buf])

    def ctx_copy(l, buf):
        return pltpu.make_async_copy(
            ybuf_ref.at[buf], yc_hbm.at[pl.ds((tile0 + l) * ROW_TILE, ROW_TILE)], fsems.at[1, buf])

    def lat_copy(l, buf):
        return pltpu.make_async_copy(
            ybuf_ref.at[buf], yl_hbm.at[pl.ds((tile0 + l - N_CTX_TILES) * ROW_TILE, ROW_TILE)],
            fsems.at[1, buf])

    def y_copy(l, buf, action):
        @pl.when(in_ctx)
        def _():
            action(ctx_copy(l, buf))

        @pl.when(jnp.logical_not(in_ctx))
        def _():
            action(lat_copy(l, buf))

    base_copy(0, 0).start()
    for l in range(TILES_PER_GROUP):
        buf = l % 2
        if l + 1 < TILES_PER_GROUP:
            base_copy(l + 1, 1 - buf).start()
        base_copy(l, buf).wait()
        if l >= 2:
            y_copy(l - 2, buf, lambda c: c.wait())
        routed = jnp.concatenate(
            [acc_vmem[pl.ds(l * ROW_TILE * SUBLANES + c, ROW_TILE, stride=LANE_CHUNKS), :]
             for c in range(LANE_CHUNKS)], axis=1)
        gate = gate_ref[pl.ds(_cond_of_tile(tile0 + l), 1), :]
        ybuf_ref[buf] = bbuf_ref[buf] + gate * routed
        y_copy(l, buf, lambda c: c.start())
    for l in range(TILES_PER_GROUP - 2, TILES_PER_GROUP):
        y_copy(l, l % 2, lambda c: c.wait())


N_WEIGHT_BUFS = 2


def _expert_call(blk_exp, n_used, next_exp, slots, ws, x_tiles, wg, wu, wd, base, gate, layer):
    any_spec = pl.BlockSpec(memory_space=pl.ANY)
    return pl.pallas_call(
        functools.partial(_expert_kernel, layer=layer),
        out_shape=STREAM_SHAPES,
        grid_spec=pltpu.PrefetchScalarGridSpec(
            num_scalar_prefetch=3,
            grid=(N_GROUPS,),
            in_specs=[any_spec] * 7 + [pl.BlockSpec((COND_ROWS, D), lambda g, *_: (0, 0))],
            out_specs=(any_spec, any_spec),
            scratch_shapes=[
                pltpu.VMEM((ACC_TILE_ROWS, LANES), F32),
                pltpu.VMEM((ACC_TILE_ROWS, LANES), F32),
                pltpu.VMEM((MOE_BM * SUBLANES, LANES), F32),
                pltpu.VMEM((MOE_BM * SUBLANES, LANES), F32),
                pltpu.VMEM((MOE_BM * SUBLANES, LANES), F32),
                pltpu.VMEM((MOE_BM * SUBLANES, LANES), F32),
                pltpu.VMEM((N_WEIGHT_BUFS, D, D_EXPERT), F32),
                pltpu.VMEM((N_WEIGHT_BUFS, D, D_EXPERT), F32),
                pltpu.VMEM((N_WEIGHT_BUFS, D_EXPERT, D), F32),
                pltpu.VMEM((D, D_EXPERT), BF16),
                pltpu.VMEM((D, D_EXPERT), BF16),
                pltpu.VMEM((D_EXPERT, D), BF16),
                pltpu.VMEM((2, ROW_TILE, D), F32),
                pltpu.VMEM((2, ROW_TILE, D), F32),
                pltpu.SMEM((SLOT_ROWS, LANES), I32),
                pltpu.SMEM((N_ASSIGN,), F32),
                pltpu.SMEM((1,), I32),
                pltpu.SemaphoreType.DMA((3,)),
                pltpu.SemaphoreType.DMA((N_WEIGHT_BUFS, 3)),
                pltpu.SemaphoreType.DMA((2, 2)),
            ]),
        compiler_params=_params(("arbitrary",)),
        name="moe_experts",
    )(blk_exp, n_used, next_exp, slots, ws, x_tiles, wg, wu, wd, base, gate)


BM_SHIFT = MOE_BM.bit_length() - 1


def _plan_kernel(cnt_ref, start_ref, n_used_ref, blk_exp_ref, next_exp_ref):
    for g in range(N_GROUPS):
        def expert(e, carry):
            run, last = carry
            blocks = (cnt_ref[g * N_EXPERTS + e] + MOE_BM - 1) >> BM_SHIFT
            start_ref[g * N_EXPERTS + e] = run << BM_SHIFT

            def mark(b, c):
                blk_exp_ref[g * MOE_NB + b] = e
                return c

            lax.fori_loop(run, run + blocks, mark, 0)
            return run + blocks, jnp.where(blocks > 0, e, last)

        used, last = lax.fori_loop(0, N_EXPERTS, expert, (0, 0))
        n_used_ref[g] = used

        def mark_unused(b, c):
            blk_exp_ref[g * MOE_NB + b] = last
            return c

        lax.fori_loop(used, MOE_NB, mark_unused, 0)

        def link(i, nxt):
            e = N_EXPERTS - 1 - i
            next_exp_ref[g * N_EXPERTS + e] = nxt
            return jnp.where(cnt_ref[g * N_EXPERTS + e] > 0, e, nxt)

        lax.fori_loop(0, N_EXPERTS, link, -1)


def _plan_call(counts):
    smem = pl.BlockSpec(memory_space=pltpu.SMEM)
    return pl.pallas_call(
        _plan_kernel,
        out_shape=(jax.ShapeDtypeStruct((N_GROUPS * N_EXPERTS,), I32),
                   jax.ShapeDtypeStruct((N_GROUPS,), I32),
                   jax.ShapeDtypeStruct((N_GROUPS * MOE_NB,), I32),
                   jax.ShapeDtypeStruct((N_GROUPS * N_EXPERTS,), I32)),
        in_specs=[smem],
        out_specs=(smem, smem, smem, smem),
        name="moe_plan",
    )(counts)


def _moe_layer(y, ada, norm_ffn, router_w, router_bias, wg, wu, wd, sg, su, sd, layer):
    x_tiles, base, keys, ws, cnts = _router_call(
        y, ada, norm_ffn, router_w, router_bias, sg, su, sd, layer)
    starts, n_used, blk_exp, next_exp = _plan_call(cnts[:, :, 0].astype(I32).reshape(-1))
    pos = _pos_call(starts, keys)
    slots = jnp.stack([
        _slot_call(pos[:, g * MOE_GROUP:(g + 1) * MOE_GROUP].reshape(-1))
        for g in range(N_GROUPS)]).reshape(N_GROUPS, SLOT_ROWS, LANES)
    ws_g = ws.reshape(TOP_K, N_GROUPS, MOE_GROUP).transpose(1, 0, 2).reshape(N_GROUPS, N_ASSIGN)
    gate = ada[layer, :, N_ADA - 1, :]
    return _expert_call(blk_exp, n_used, next_exp, slots, ws_g, x_tiles, wg, wu, wd, base, gate, layer)


def kernel(x_prompt, x_sample, cache_k, cache_v, c, c_ctx, ada_w, ada_b, norm_mix, norm_ffn,
           conv_w1, conv_b1, conv_dw, conv_dw_b, conv_norm, conv_w2, conv_b2,
           attn_wqkv, attn_q_norm, attn_k_norm, attn_sink, attn_wo,
           router_w, router_bias, exp_w_gate, exp_w_up, exp_w_down,
           sh_w_gate, sh_w_up, sh_w_down):
    y = (x_prompt.reshape(TP, D), x_sample.reshape(TS, D))
    cond = jnp.concatenate(
        [c_ctx[None, :], c, jnp.zeros((COND_ROWS - N_COND, D), F32)], axis=0)
    ada = _ada_call(cond, ada_w, ada_b).reshape(DEPTH, COND_ROWS, N_ADA, D)

    new_k = new_v = None
    for layer in range(DEPTH):
        j = layer // 2
        if layer % 2 == 0:
            u = _conv_in_call(y, ada, norm_mix, conv_w1, conv_b1, layer, j)
            y = _conv_out_call(u, y, ada, conv_dw, conv_dw_b, conv_norm, conv_w2, conv_b2, layer, j)
        else:
            q, k, v, kt, vt = _qkv_call(y, ada, norm_mix, attn_wqkv, attn_q_norm, attn_k_norm, layer, j)
            cache_shape = (N_CTX_SEQ, 1, N_KV, HEAD_DIM, CTX_LEN)
            new_k = kt.reshape(cache_shape).transpose(0, 1, 4, 2, 3)
            new_v = vt.reshape(cache_shape).transpose(0, 1, 4, 2, 3)
            y = (_attn_ctx_call(q, k, v, attn_sink, attn_wo, y[0], ada, layer, j),
                 _attn_lat_call(q, k, v, cache_k, cache_v, attn_sink, attn_wo, y[1], ada, layer, j))
        y = _moe_layer(y, ada, norm_ffn, router_w, router_bias, exp_w_gate, exp_w_up, exp_w_down,
                       sh_w_gate, sh_w_up, sh_w_down, layer)

    y_p = y[0].reshape(N_CTX_SEQ, CTX_LEN, D)
    y_s = y[1].reshape(N_LAT_SEQ, LAT_LEN, D)
    return (y_p, y_s, new_k, new_v)
```

```python
import functools

import jax
import jax.numpy as jnp
import numpy as np
from jax import lax
from jax.experimental import pallas as pl
from jax.experimental.pallas import tpu as pltpu

F32 = jnp.float32
BF16 = jnp.bfloat16
I32 = jnp.int32

D = 1024
N_CTX_SEQ = 32
CTX_LEN = 256
N_LAT_SEQ = 4
LAT_LEN = 1024
TP = N_CTX_SEQ * CTX_LEN
TS = N_LAT_SEQ * LAT_LEN
T = TP + TS
DEPTH = 2
N_ADA = 6
N_COND = 1 + N_LAT_SEQ
COND_ROWS = 8
CONV_WIDTH = 31
CONV_PAD = CONV_WIDTH // 2
N_HEADS = 16
N_KV = 4
HEAD_DIM = 64
GQA = N_HEADS // N_KV
KV_WIDTH = N_KV * HEAD_DIM
QKV_WIDTH = D + 2 * KV_WIDTH
WINDOW = 128
GRID_W = 64
ROPE_PAIRS = HEAD_DIM // 4
ROPE_THETA = 10000.0
N_EXPERTS = 64
N_EGROUPS = 8
EGROUP = N_EXPERTS // N_EGROUPS
TOPK_GROUPS = 4
TOP_K = 8
D_EXPERT = 256
D_SHARED = 256
ROUTED_SCALE = 2.5
NORM_EPS = 1e-6

LANES = 128
SUBLANES = 8
VMEM_LIMIT = 56 * 1024 * 1024

ROW_TILE = 256
N_TILES = T // ROW_TILE
N_CTX_TILES = TP // ROW_TILE
LAT_TILES_PER_SEQ = LAT_LEN // ROW_TILE
HALO = 16
LANE_CHUNKS = D // LANES

MOE_GROUP = 4096
N_GROUPS = T // MOE_GROUP
N_CTX_GROUPS = TP // MOE_GROUP
TILES_PER_GROUP = MOE_GROUP // ROW_TILE
MOE_BM = 256
MOE_NB = MOE_GROUP * TOP_K // MOE_BM + N_EXPERTS
N_ASSIGN = MOE_GROUP * TOP_K
ACC_ROWS = MOE_GROUP + SUBLANES
RANK_BITS = 12
Q_BLOCK = 128


def _cond_of_tile(i):
    return jnp.where(i < N_CTX_TILES, 0, 1 + (i - N_CTX_TILES) // LAT_TILES_PER_SEQ)


def _params(sem, vmem=VMEM_LIMIT):
    return pltpu.CompilerParams(dimension_semantics=sem, vmem_limit_bytes=vmem)


def _split(a):
    hi = a.astype(BF16)
    lo = (a - hi.astype(F32)).astype(BF16)
    return hi, lo


def _dot3(a, b):
    a_hi, a_lo = _split(a)
    b_hi, b_lo = _split(b)
    d = functools.partial(jnp.dot, preferred_element_type=F32)
    return d(a_hi, b_hi) + d(a_lo, b_hi) + d(a_hi, b_lo)


def _dot2(a, b_bf16):
    a_hi, a_lo = _split(a)
    d = functools.partial(jnp.dot, preferred_element_type=F32)
    return d(a_hi, b_bf16) + d(a_lo, b_bf16)


def _rms(x, g):
    return x * lax.rsqrt(jnp.mean(x * x, axis=-1, keepdims=True) + NORM_EPS) * g


def _modulate(x, g, shift, scale):
    return _rms(x, g) * (1.0 + scale) + shift


def _silu(x):
    return x * jax.nn.sigmoid(x)


def _ada_spec(layer):
    return pl.BlockSpec((None, None, N_ADA, D), lambda i: (layer, _cond_of_tile(i), 0, 0))


def _tile_spec(width=D):
    return pl.BlockSpec((ROW_TILE, width), lambda i: (i, 0))


STREAM_SPECS = [
    pl.BlockSpec((ROW_TILE, D), lambda i: (jnp.minimum(i, N_CTX_TILES - 1), 0)),
    pl.BlockSpec((ROW_TILE, D), lambda i: (jnp.maximum(i - N_CTX_TILES, 0), 0)),
]
STREAM_SHAPES = (jax.ShapeDtypeStruct((TP, D), F32), jax.ShapeDtypeStruct((TS, D), F32))


def _read_stream(ctx_ref, lat_ref):
    return jnp.where(pl.program_id(0) < N_CTX_TILES, ctx_ref[...], lat_ref[...])


def _write_stream(ctx_ref, lat_ref, value):
    i = pl.program_id(0)

    @pl.when(i < N_CTX_TILES)
    def _():
        ctx_ref[...] = value

    @pl.when(i >= N_CTX_TILES)
    def _():
        lat_ref[...] = value


def _const_spec(shape):
    nd = len(shape)
    return pl.BlockSpec(shape, lambda *_: (0,) * nd)


def _layer_spec(layer, shape):
    nd = len(shape)
    return pl.BlockSpec((None,) + tuple(shape), lambda *_: (layer,) + (0,) * nd)


ADA_NB = 1536


def _ada_kernel(c_ref, w_ref, b_ref, o_ref):
    o_ref[...] = _dot3(_silu(c_ref[...]), w_ref[...]) + b_ref[...]


def _ada_call(cond, ada_w, ada_b):
    return pl.pallas_call(
        _ada_kernel,
        out_shape=jax.ShapeDtypeStruct((DEPTH, COND_ROWS, N_ADA * D), F32),
        grid=(DEPTH, N_ADA * D // ADA_NB),
        in_specs=[
            pl.BlockSpec((COND_ROWS, D), lambda l, n: (0, 0)),
            pl.BlockSpec((None, D, ADA_NB), lambda l, n: (l, 0, n)),
            pl.BlockSpec((None, 1, ADA_NB), lambda l, n: (l, 0, n)),
        ],
        out_specs=pl.BlockSpec((None, COND_ROWS, ADA_NB), lambda l, n: (l, 0, n)),
        compiler_params=_params(("arbitrary", "arbitrary")),
        name="ada_params",
    )(cond, ada_w, ada_b.reshape(DEPTH, 1, N_ADA * D))


def _conv_in_kernel(xc_ref, xl_ref, ada_ref, g_ref, w1_ref, b1_ref, u_ref, w1b_ref):
    @pl.when(pl.program_id(0) == 0)
    def _():
        w1b_ref[...] = w1_ref[...].astype(BF16)

    h = _modulate(_read_stream(xc_ref, xl_ref), g_ref[...], ada_ref[0:1, :], ada_ref[1:2, :])
    u = jnp.dot(h.astype(BF16), w1b_ref[...], preferred_element_type=F32) + b1_ref[...]
    u_ref[...] = u[:, :D] * jax.nn.sigmoid(u[:, D:])


def _conv_in_call(y, ada, norm_mix, conv_w1, conv_b1, layer, j):
    return pl.pallas_call(
        _conv_in_kernel,
        out_shape=jax.ShapeDtypeStruct((T, D), F32),
        grid=(N_TILES,),
        in_specs=STREAM_SPECS + [
            _ada_spec(layer),
            _layer_spec(layer, (1, D)),
            _layer_spec(j, (D, 2 * D)),
            _layer_spec(j, (1, 2 * D)),
        ],
        out_specs=_tile_spec(),
        scratch_shapes=[pltpu.VMEM((D, 2 * D), BF16)],
        compiler_params=_params(("arbitrary",)),
        name="conv_in",
    )(*y, ada, norm_mix.reshape(DEPTH, 1, D), conv_w1, conv_b1.reshape(-1, 1, 2 * D))


CONV_ROWS = 64
BUF_ROWS = ROW_TILE + 2 * HALO


def _conv_out_kernel(u_ref, up_ref, un_ref, yc_ref, yl_ref, ada_ref, dw_ref, dwb_ref, gn_ref,
                     w2_ref, b2_ref, oc_ref, ol_ref, buf_ref, z_ref, w2b_ref):
    i = pl.program_id(0)

    @pl.when(i == 0)
    def _():
        w2b_ref[...] = w2_ref[...].astype(BF16)

    lat = i >= N_CTX_TILES
    pos = jnp.where(lat, i - N_CTX_TILES, 0) % LAT_TILES_PER_SEQ
    has_prev = lat & (pos != 0)
    has_next = lat & (pos != LAT_TILES_PER_SEQ - 1)
    for c in range(LANE_CHUNKS):
        cs = slice(c * LANES, (c + 1) * LANES)
        buf_ref[c, 0:HALO, :] = jnp.where(has_prev, up_ref[:, cs], 0.0)
        buf_ref[c, HALO:HALO + ROW_TILE, :] = u_ref[:, cs]
        buf_ref[c, HALO + ROW_TILE:BUF_ROWS, :] = jnp.where(has_next, un_ref[:, cs], 0.0)

    off = HALO - CONV_PAD
    for c in range(LANE_CHUNKS):
        cs = slice(c * LANES, (c + 1) * LANES)
        for r0 in range(0, ROW_TILE, CONV_ROWS):
            acc = jnp.broadcast_to(dwb_ref[:, cs], (CONV_ROWS, LANES))
            for k in range(CONV_WIDTH):
                win = buf_ref[c, r0 + k + off:r0 + k + off + CONV_ROWS, :]
                acc = acc + dw_ref[k:k + 1, cs] * win
            z_ref[r0:r0 + CONV_ROWS, cs] = acc

    z = _silu(_rms(z_ref[...], gn_ref[...]))
    m = jnp.dot(z.astype(BF16), w2b_ref[...], preferred_element_type=F32) + b2_ref[...]
    _write_stream(oc_ref, ol_ref, _read_stream(yc_ref, yl_ref) + ada_ref[2:3, :] * m)


def _conv_out_call(u, y, ada, conv_dw, conv_dw_b, conv_norm, conv_w2, conv_b2, layer, j):
    halos_per_tile = ROW_TILE // HALO
    last_halo = T // HALO - 1
    return pl.pallas_call(
        _conv_out_kernel,
        out_shape=STREAM_SHAPES,
        grid=(N_TILES,),
        in_specs=[
            _tile_spec(),
            pl.BlockSpec((HALO, D), lambda i: (jnp.maximum(i * halos_per_tile - 1, 0), 0)),
            pl.BlockSpec((HALO, D), lambda i: (jnp.minimum((i + 1) * halos_per_tile, last_halo), 0)),
        ] + STREAM_SPECS + [
            _ada_spec(layer),
            _layer_spec(j, (CONV_WIDTH + 1, D)),
            _layer_spec(j, (1, D)),
            _layer_spec(j, (1, D)),
            _layer_spec(j, (D, D)),
            _layer_spec(j, (1, D)),
        ],
        out_specs=STREAM_SPECS,
        scratch_shapes=[pltpu.VMEM((LANE_CHUNKS, BUF_ROWS, LANES), F32), pltpu.VMEM((ROW_TILE, D), F32),
                        pltpu.VMEM((D, D), BF16)],
        compiler_params=_params(("arbitrary",)),
        name="conv_out",
    )(u, u, u, *y, ada, jnp.pad(conv_dw, ((0, 0), (0, 1), (0, 0))), conv_dw_b.reshape(-1, 1, D),
      conv_norm.reshape(-1, 1, D), conv_w2, conv_b2.reshape(-1, 1, D))


def _swap_pairs(x, width):
    lane = lax.broadcasted_iota(I32, x.shape, 1)
    first = (lane % (2 * ROPE_PAIRS)) < ROPE_PAIRS
    return jnp.where(first, pltpu.roll(x, width - ROPE_PAIRS, 1), pltpu.roll(x, ROPE_PAIRS, 1))


def _qkv_kernel(xc_ref, xl_ref, ada_ref, g_ref, w_ref, qn_ref, kn_ref, cos_ref, sin_ref,
                q_ref, k_ref, v_ref, kt_ref, vt_ref, wb_ref, hsum_ref, hexp_ref):
    i = pl.program_id(0)

    @pl.when(i == 0)
    def _():
        wb_ref[...] = w_ref[...].astype(BF16)
        head_of_row = lax.broadcasted_iota(I32, (D, LANES), 0) // HEAD_DIM
        head_col = lax.broadcasted_iota(I32, (D, LANES), 1)
        hsum_ref[...] = jnp.where(head_of_row == head_col, 1.0 / HEAD_DIM, 0.0).astype(BF16)
        head_row = lax.broadcasted_iota(I32, (LANES, D), 0)
        head_of_col = lax.broadcasted_iota(I32, (LANES, D), 1) // HEAD_DIM
        hexp_ref[...] = jnp.where(head_row == head_of_col, 1.0, 0.0).astype(BF16)

    h = _modulate(_read_stream(xc_ref, xl_ref), g_ref[...], ada_ref[0:1, :], ada_ref[1:2, :])
    qkv = jnp.dot(h.astype(BF16), wb_ref[...], preferred_element_type=F32)
    q = qkv[:, :D]
    k = qkv[:, D:D + KV_WIDTH]
    v_ref[...] = qkv[:, D + KV_WIDTH:].astype(BF16)
    q_ms = _dot2(_dot2(q * q, hsum_ref[...]), hexp_ref[...])
    k_ms = _dot2(_dot2(k * k, hsum_ref[0:KV_WIDTH, :]), hexp_ref[:, 0:KV_WIDTH])
    qn = q * lax.rsqrt(q_ms + NORM_EPS) * qn_ref[...]
    kn = k * lax.rsqrt(k_ms + NORM_EPS) * kn_ref[...]

    @pl.when(i < N_CTX_TILES)
    def _():
        q_ref[...] = qn.astype(BF16)
        k_ref[...] = kn.astype(BF16)
        kt_ref[...] = kn.T
        vt_ref[...] = qkv[:, D + KV_WIDTH:].T

    @pl.when(i >= N_CTX_TILES)
    def _():
        cos = cos_ref[...]
        sin = sin_ref[...]
        q_ref[...] = (qn * cos + _swap_pairs(qn, D) * sin).astype(BF16)
        k_ref[...] = (kn * cos[:, :KV_WIDTH]
                      + _swap_pairs(kn, KV_WIDTH) * sin[:, :KV_WIDTH]).astype(BF16)


def _rope_tables():
    pos = np.arange(LAT_LEN)
    row = (pos // GRID_W).astype(np.float32)
    col = (pos % GRID_W).astype(np.float32)
    inv_freq = np.float32(ROPE_THETA) ** (-np.arange(ROPE_PAIRS, dtype=np.float32) / ROPE_PAIRS)
    d = np.arange(D) % HEAD_DIM
    freq = inv_freq[d % ROPE_PAIRS].astype(np.float32)
    p = np.where((d >= HEAD_DIM // 2)[None, :], col[:, None], row[:, None])
    ang = (p * freq[None, :]).astype(np.float32)
    first = (d % (2 * ROPE_PAIRS)) < ROPE_PAIRS
    cos = np.cos(ang).astype(np.float32)
    sin = np.sin(ang).astype(np.float32)
    return jnp.asarray(cos), jnp.asarray(np.where(first[None, :], -sin, sin))


def _qkv_call(y, ada, norm_mix, wqkv, q_norm, k_norm, layer, j):
    cos, sin = _rope_tables()
    table_spec = pl.BlockSpec(
        (ROW_TILE, D), lambda i: (jnp.maximum(i - N_CTX_TILES, 0) % LAT_TILES_PER_SEQ, 0))
    cache_spec = pl.BlockSpec((None, KV_WIDTH, CTX_LEN), lambda i: (jnp.minimum(i, N_CTX_SEQ - 1), 0, 0))
    return pl.pallas_call(
        _qkv_kernel,
        out_shape=(jax.ShapeDtypeStruct((T, D), BF16), jax.ShapeDtypeStruct((T, KV_WIDTH), BF16),
                   jax.ShapeDtypeStruct((T, KV_WIDTH), BF16),
                   jax.ShapeDtypeStruct((N_CTX_SEQ, KV_WIDTH, CTX_LEN), F32),
                   jax.ShapeDtypeStruct((N_CTX_SEQ, KV_WIDTH, CTX_LEN), F32)),
        grid=(N_TILES,),
        in_specs=STREAM_SPECS + [
            _ada_spec(layer),
            _layer_spec(layer, (1, D)),
            _layer_spec(j, (D, QKV_WIDTH)),
            _const_spec((1, D)),
            _const_spec((1, KV_WIDTH)),
            table_spec,
            table_spec,
        ],
        out_specs=(_tile_spec(), _tile_spec(KV_WIDTH), _tile_spec(KV_WIDTH), cache_spec, cache_spec),
        scratch_shapes=[pltpu.VMEM((D, QKV_WIDTH), BF16), pltpu.VMEM((D, LANES), BF16),
                        pltpu.VMEM((LANES, D), BF16)],
        compiler_params=_params(("arbitrary",)),
        name="qkv",
    )(*y, ada, norm_mix.reshape(DEPTH, 1, D), wqkv,
      jnp.tile(q_norm[j], N_HEADS).reshape(1, D), jnp.tile(k_norm[j], N_KV).reshape(1, KV_WIDTH),
      cos, sin)


HEADS_PER_TILE = LANES // HEAD_DIM
TILES_PER_KV = GQA // HEADS_PER_TILE


def _gqa_attention(q, segments, sink_ref, rows):
    d = functools.partial(jnp.dot, preferred_element_type=F32)
    half = lax.broadcasted_iota(I32, (rows, LANES), 1) < HEAD_DIM
    q_zero = jnp.zeros((rows, LANES), BF16)

    scores, sinks, kvs = [], [], []
    for kv in range(N_KV):
        sl = slice(kv * HEAD_DIM, (kv + 1) * HEAD_DIM)
        parts = []
        for t in range(TILES_PER_KV):
            tile = kv * TILES_PER_KV + t
            qt = q[:, tile * LANES:(tile + 1) * LANES]
            parts += [jnp.where(half, qt, q_zero), jnp.where(half, q_zero, qt)]
        qg = jnp.concatenate(parts, axis=0)
        seg_scores, seg_kv = [], []
        for k, v, bias in segments:
            kk = jnp.concatenate([k[:, sl], k[:, sl]], axis=1)
            s = lax.dot_general(qg, kk, (((1,), (1,)), ((), ())),
                                preferred_element_type=F32) * (HEAD_DIM ** -0.5)
            seg_scores.append(s if bias is None else s + bias)
            v_zero = jnp.zeros_like(v[:, sl])
            seg_kv.append((jnp.concatenate([v[:, sl], v_zero], axis=1),
                           jnp.concatenate([v_zero, v[:, sl]], axis=1)))
        scores.append(seg_scores)
        kvs.append(seg_kv)
        sinks.append(jnp.concatenate(
            [jnp.broadcast_to(sink_ref[0:1, kv * GQA + g:kv * GQA + g + 1], (rows, 1))
             for g in range(GQA)], axis=0))

    maxes = []
    for kv in range(N_KV):
        m = sinks[kv]
        for s in scores[kv]:
            m = jnp.maximum(m, jnp.max(s, axis=1, keepdims=True))
        maxes.append(m)
    probs = [[jnp.exp(s - maxes[kv]).astype(BF16) for s in scores[kv]] for kv in range(N_KV)]

    tiles = []
    for kv in range(N_KV):
        den = jnp.exp(sinks[kv] - maxes[kv])
        for p in probs[kv]:
            den = den + d(p, jnp.ones((p.shape[1], LANES), BF16))
        inv = 1.0 / den
        for t in range(TILES_PER_KV):
            tile = None
            for h in range(HEADS_PER_TILE):
                g = t * HEADS_PER_TILE + h
                rs = slice(g * rows, (g + 1) * rows)
                o = None
                for p, v_halves in zip(probs[kv], kvs[kv]):
                    part = d(p[rs, :], v_halves[h])
                    o = part if o is None else o + part
                o = o * inv[rs, :]
                tile = o if tile is None else tile + o
            tiles.append(tile)
    return jnp.concatenate(tiles, axis=1)


def _attn_ctx_kernel(q_ref, k_ref, v_ref, sink_ref, wo_ref, y_ref, ada_ref, o_ref, wob_ref):
    @pl.when(pl.program_id(0) == 0)
    def _():
        wob_ref[...] = wo_ref[...].astype(BF16)

    segments = [(k_ref[...].astype(BF16), v_ref[...].astype(BF16), None)]
    o = _gqa_attention(q_ref[...].astype(BF16), segments, sink_ref, CTX_LEN)
    m_out = jnp.dot(o.astype(BF16), wob_ref[...], preferred_element_type=F32)
    o_ref[...] = y_ref[...] + ada_ref[2:3, :] * m_out


def _attn_ctx_call(q, k, v, sink, wo, y_ctx, ada, layer, j):
    return pl.pallas_call(
        _attn_ctx_kernel,
        out_shape=STREAM_SHAPES[0],
        grid=(N_CTX_SEQ,),
        in_specs=[
            _tile_spec(), _tile_spec(KV_WIDTH), _tile_spec(KV_WIDTH),
            _layer_spec(j, (1, N_HEADS)),
            _layer_spec(j, (D, D)),
            _tile_spec(),
            _ada_spec(layer),
        ],
        out_specs=_tile_spec(),
        scratch_shapes=[pltpu.VMEM((D, D), BF16)],
        compiler_params=_params(("arbitrary",)),
        name="attn_ctx",
    )(q, k, v, sink.reshape(-1, 1, N_HEADS), wo, y_ctx, ada)


LAT_WIN = 3 * WINDOW
N_QBLOCKS = LAT_LEN // Q_BLOCK


def _attn_lat_kernel(q_ref, k_ref, v_ref, kc_ref, vc_ref, sink_ref, wo_ref, y_ref, ada_ref,
                     o_ref, wob_ref):
    b = pl.program_id(0)
    n = pl.program_id(1)

    @pl.when((b == 0) & (n == 0))
    def _():
        wob_ref[...] = wo_ref[...].astype(BF16)

    start = pl.multiple_of(jnp.clip((n - 1) * WINDOW, 0, LAT_LEN - LAT_WIN), WINDOW)
    kw = k_ref[pl.ds(start, LAT_WIN), :].astype(BF16)
    vw = v_ref[pl.ds(start, LAT_WIN), :].astype(BF16)
    qpos = n * Q_BLOCK + lax.broadcasted_iota(I32, (GQA * Q_BLOCK, LAT_WIN), 0) % Q_BLOCK
    kpos = start + lax.broadcasted_iota(I32, (GQA * Q_BLOCK, LAT_WIN), 1)
    band = jnp.where(jnp.abs(kpos - qpos) <= WINDOW, 0.0, -jnp.inf)
    segments = [(kw, vw, band),
                (kc_ref[...].astype(BF16), vc_ref[...].astype(BF16), None)]
    o = _gqa_attention(q_ref[...].astype(BF16), segments, sink_ref, Q_BLOCK)
    m_out = jnp.dot(o.astype(BF16), wob_ref[...], preferred_element_type=F32)
    o_ref[...] = y_ref[...] + ada_ref[2:3, :] * m_out


def _attn_lat_call(q, k, v, cache_k, cache_v, sink, wo, y_lat, ada, layer, j):
    q_row0 = TP // Q_BLOCK
    seq0 = TP // LAT_LEN
    qspec = pl.BlockSpec((Q_BLOCK, D), lambda b, n: (q_row0 + b * N_QBLOCKS + n, 0))
    yspec = pl.BlockSpec((Q_BLOCK, D), lambda b, n: (b * N_QBLOCKS + n, 0))
    kvspec = pl.BlockSpec((LAT_LEN, KV_WIDTH), lambda b, n: (seq0 + b, 0))
    cspec = pl.BlockSpec((None, None, CTX_LEN, KV_WIDTH), lambda b, n: (b, j, 0, 0))
    ck = cache_k.reshape(N_LAT_SEQ, -1, CTX_LEN, KV_WIDTH)
    cv = cache_v.reshape(N_LAT_SEQ, -1, CTX_LEN, KV_WIDTH)
    return pl.pallas_call(
        _attn_lat_kernel,
        out_shape=STREAM_SHAPES[1],
        grid=(N_LAT_SEQ, N_QBLOCKS),
        in_specs=[
            qspec, kvspec, kvspec, cspec, cspec,
            _layer_spec(j, (1, N_HEADS)),
            _layer_spec(j, (D, D)),
            yspec,
            pl.BlockSpec((None, None, N_ADA, D), lambda b, n: (layer, 1 + b, 0, 0)),
        ],
        out_specs=yspec,
        scratch_shapes=[pltpu.VMEM((D, D), BF16)],
        compiler_params=_params(("arbitrary", "arbitrary")),
        name="attn_lat",
    )(q, k, v, ck, cv, sink.reshape(-1, 1, N_HEADS), wo, y_lat, ada)


def _router_kernel(yc_ref, yl_ref, ada_ref, g_ref, rw_ref, rb_ref, sg_ref, su_ref, sd_ref,
                   h_ref, base_ref, key_ref, w_ref, cnt_ref,
                   carry_ref, tri_ref, sgb_ref, sub_ref, sdb_ref):
    i = pl.program_id(0)

    @pl.when(i == 0)
    def _():
        sgb_ref[...] = sg_ref[...].astype(BF16)
        sub_ref[...] = su_ref[...].astype(BF16)
        sdb_ref[...] = sd_ref[...].astype(BF16)
        r = lax.broadcasted_iota(I32, (ROW_TILE, ROW_TILE), 0)
        c = lax.broadcasted_iota(I32, (ROW_TILE, ROW_TILE), 1)
        tri_ref[...] = jnp.where(r < c, 1.0, 0.0).astype(BF16)

    @pl.when(i % TILES_PER_GROUP == 0)
    def _():
        carry_ref[...] = jnp.zeros_like(carry_ref)

    y = _read_stream(yc_ref, yl_ref)
    h = _modulate(y, g_ref[...], ada_ref[3:4, :], ada_ref[4:5, :])
    for c in range(LANE_CHUNKS):
        h_ref[pl.ds(c, ROW_TILE, stride=LANE_CHUNKS), :] = h[:, c * LANES:(c + 1) * LANES]
    hb = h.astype(BF16)

    d = functools.partial(jnp.dot, preferred_element_type=F32)
    act = (_silu(d(hb, sgb_ref[...])) * d(hb, sub_ref[...])).astype(BF16)
    base_ref[...] = y + ada_ref[5:6, :] * d(act, sdb_ref[...])

    logits = _dot3(h, rw_ref[...])
    logits_t = jnp.concatenate([logits, jnp.zeros_like(logits)], axis=1).T[:N_EXPERTS]
    scores = jax.nn.sigmoid(logits_t)
    shape3 = (N_EGROUPS, EGROUP, ROW_TILE)
    scores3 = scores.reshape(shape3)
    choice3 = (scores + rb_ref[...]).reshape(shape3)
    sub = lax.broadcasted_iota(I32, shape3, 1)
    eidx = lax.broadcasted_iota(I32, shape3, 0) * EGROUP + sub
    gidx = lax.broadcasted_iota(I32, (N_EGROUPS, 1, ROW_TILE), 0)
    neg = -jnp.inf

    m1 = jnp.max(choice3, axis=1, keepdims=True)
    i1 = jnp.min(jnp.where(choice3 == m1, sub, EGROUP), axis=1, keepdims=True)
    m2 = jnp.max(jnp.where(sub == i1, neg, choice3), axis=1, keepdims=True)
    gscore = m1 + m2
    allowed = jnp.zeros((N_EGROUPS, 1, ROW_TILE), jnp.bool_)
    for _ in range(TOPK_GROUPS):
        gm = jnp.max(gscore, axis=0, keepdims=True)
        gi = jnp.min(jnp.where(gscore == gm, gidx, N_EGROUPS), axis=0, keepdims=True)
        hit = gidx == gi
        allowed = allowed | hit
        gscore = jnp.where(hit, neg, gscore)

    def reduce_experts(fn, x):
        return fn(fn(x, axis=0, keepdims=True), axis=1, keepdims=True)

    cm = jnp.where(allowed, choice3, neg)
    picked = jnp.zeros(shape3, F32)
    e_rows, s_rows = [], []
    for _ in range(TOP_K):
        m = reduce_experts(jnp.max, cm)
        ik = reduce_experts(jnp.min, jnp.where(cm == m, eidx, N_EXPERTS))
        hit = eidx == ik
        e_rows.append(ik)
        s_rows.append(reduce_experts(jnp.sum, jnp.where(hit, scores3, 0.0)))
        cm = jnp.where(hit, neg, cm)
        picked = jnp.where(hit, 1.0, picked)

    picked2 = picked.reshape(N_EXPERTS, ROW_TILE)
    cum3 = (d(picked2.astype(BF16), tri_ref[...]) + carry_ref[...]).reshape(shape3)
    carry_ref[...] = carry_ref[...] + jnp.sum(picked2, axis=1, keepdims=True)
    cnt_ref[...] = jnp.broadcast_to(carry_ref[...], (N_EXPERTS, LANES))

    denom = functools.reduce(lambda a, b: a + b, s_rows)
    key_rows, w_rows = [], []
    for kk in range(TOP_K):
        rank = reduce_experts(jnp.sum, jnp.where(eidx == e_rows[kk], cum3, 0.0)).astype(I32)
        key_rows.append(((e_rows[kk] << RANK_BITS) | rank).reshape(1, ROW_TILE))
        w_rows.append((s_rows[kk] / denom * ROUTED_SCALE).reshape(1, ROW_TILE))
    key_ref[...] = jnp.concatenate(key_rows, axis=0)
    w_ref[...] = jnp.concatenate(w_rows, axis=0)


def _router_call(y, ada, norm_ffn, router_w, router_bias, sg, su, sd, layer):
    return pl.pallas_call(
        _router_kernel,
        out_shape=(
            jax.ShapeDtypeStruct((N_GROUPS, MOE_GROUP * LANE_CHUNKS, LANES), F32),
            jax.ShapeDtypeStruct((T, D), F32),
            jax.ShapeDtypeStruct((TOP_K, T), I32),
            jax.ShapeDtypeStruct((TOP_K, T), F32),
            jax.ShapeDtypeStruct((N_GROUPS, N_EXPERTS, LANES), F32),
        ),
        grid=(N_TILES,),
        in_specs=STREAM_SPECS + [
            _ada_spec(layer),
            _layer_spec(layer, (1, D)),
            _layer_spec(layer, (D, N_EXPERTS)),
            _layer_spec(layer, (N_EXPERTS, 1)),
            _layer_spec(layer, (D, D_SHARED)),
            _layer_spec(layer, (D, D_SHARED)),
            _layer_spec(layer, (D_SHARED, D)),
        ],
        out_specs=(
            pl.BlockSpec((None, ROW_TILE * LANE_CHUNKS, LANES),
                         lambda i: (i // TILES_PER_GROUP, i % TILES_PER_GROUP, 0)),
            _tile_spec(),
            pl.BlockSpec((TOP_K, ROW_TILE), lambda i: (0, i)),
            pl.BlockSpec((TOP_K, ROW_TILE), lambda i: (0, i)),
            pl.BlockSpec((None, N_EXPERTS, LANES), lambda i: (i // TILES_PER_GROUP, 0, 0)),
        ),
        scratch_shapes=[
            pltpu.VMEM((N_EXPERTS, 1), F32),
            pltpu.VMEM((ROW_TILE, ROW_TILE), BF16),
            pltpu.VMEM((D, D_SHARED), BF16),
            pltpu.VMEM((D, D_SHARED), BF16),
            pltpu.VMEM((D_SHARED, D), BF16),
        ],
        compiler_params=_params(("arbitrary",)),
        name="router",
    )(*y, ada, norm_ffn.reshape(DEPTH, 1, D), router_w, router_bias.reshape(DEPTH, N_EXPERTS, 1),
      sg, su, sd)


SLOT_WORDS = (MOE_NB + 3) * MOE_BM
SLOT_ROWS = SLOT_WORDS // LANES


def _pos_kernel(start_ref, key_ref, pos_ref):
    g = pl.program_id(0)
    key = key_ref[...]
    e = key >> RANK_BITS
    pos = key & ((1 << RANK_BITS) - 1)
    for ee in range(N_EXPERTS):
        pos = pos + jnp.where(e == ee, start_ref[g * N_EXPERTS + ee], 0)
    pos_ref[...] = pos


def _pos_call(starts, keys):
    spec = pl.BlockSpec((TOP_K, MOE_GROUP), lambda g, s: (0, g))
    return pl.pallas_call(
        _pos_kernel,
        out_shape=jax.ShapeDtypeStruct((TOP_K, T), I32),
        grid_spec=pltpu.PrefetchScalarGridSpec(
            num_scalar_prefetch=1, grid=(N_GROUPS,), in_specs=[spec], out_specs=spec),
        compiler_params=_params(("arbitrary",)),
        name="moe_pos",
    )(starts, keys)


SLOT_ROW_BITS = 16
DUMMY_SLOT = MOE_GROUP * SUBLANES
SLOT_UNROLL = 4


def _slot_kernel(pos_ref, init_hbm, slot_ref, sem):
    fill = pltpu.make_async_copy(init_hbm, slot_ref, sem)
    fill.start()
    fill.wait()
    per_token = SUBLANES + (1 << SLOT_ROW_BITS)

    def place(t, carry):
        word = t * per_token
        for k in range(TOP_K):
            slot_ref[pos_ref[k * MOE_GROUP + t]] = word + ((k * MOE_GROUP) << SLOT_ROW_BITS)
        return carry

    lax.fori_loop(0, MOE_GROUP, place, 0, unroll=SLOT_UNROLL)


def _slot_call(pos):
    smem = pl.BlockSpec(memory_space=pltpu.SMEM)
    return pl.pallas_call(
        _slot_kernel,
        out_shape=jax.ShapeDtypeStruct((SLOT_WORDS,), I32),
        in_specs=[smem, pl.BlockSpec(memory_space=pl.ANY)],
        out_specs=smem,
        scratch_shapes=[pltpu.SemaphoreType.DMA],
        name="moe_slots",
    )(pos, jnp.full((SLOT_WORDS,), DUMMY_SLOT, I32))


TILE_ROWS = MOE_GROUP * SUBLANES
ACC_TILE_ROWS = ACC_ROWS * SUBLANES
RMW_ROWS = 16


def _expert_kernel(blk_exp_ref, n_used_ref, next_exp_ref,
                   slot_hbm, w_hbm, x_hbm, wg_hbm, wu_hbm, wd_hbm, base_hbm, gate_ref,
                   yc_hbm, yl_hbm,
                   x_vmem, acc_vmem, gat0_ref, gat1_ref, yt0_ref, yt1_ref,
                   wgf_ref, wuf_ref, wdf_ref, wgb_ref, wub_ref, wdb_ref, bbuf_ref, ybuf_ref,
                   slot_smem, w_smem, turn_ref, sems, wsems, fsems, *, layer):
    g = pl.program_id(0)
    rows_per_block = MOE_BM // LANES

    def slot_word(block, r):
        return slot_smem[block * rows_per_block + r // LANES, r % LANES]

    def tile_rows(word):
        return pl.ds(pl.multiple_of(word & ((1 << SLOT_ROW_BITS) - 1), SUBLANES), SUBLANES)

    def gather(block, gat_ref):
        for r in range(MOE_BM):
            gat_ref[pl.ds(r * SUBLANES, SUBLANES), :] = x_vmem[tile_rows(slot_word(block, r)), :]

    def expert_mlp(gat_ref, yt_ref):
        xs = jnp.concatenate(
            [gat_ref[pl.ds(c, MOE_BM, stride=LANE_CHUNKS), :] for c in range(LANE_CHUNKS)],
            axis=1).astype(BF16)
        d = functools.partial(jnp.dot, preferred_element_type=F32)
        act = (_silu(d(xs, wgb_ref[...])) * d(xs, wub_ref[...])).astype(BF16)
        yb = d(act, wdb_ref[...])
        for c in range(LANE_CHUNKS):
            yt_ref[pl.ds(c, MOE_BM, stride=LANE_CHUNKS), :] = yb[:, c * LANES:(c + 1) * LANES]

    def combine(block, yt_ref):
        for r0 in range(0, MOE_BM, RMW_ROWS):
            words = [slot_word(block, r) for r in range(r0, r0 + RMW_ROWS)]
            new = [acc_vmem[tile_rows(word), :]
                   + w_smem[word >> SLOT_ROW_BITS] * yt_ref[pl.ds(r * SUBLANES, SUBLANES), :]
                   for r, word in zip(range(r0, r0 + RMW_ROWS), words)]
            for word, tile in zip(words, new):
                acc_vmem[tile_rows(word), :] = tile

    def group_copies():
        return (pltpu.make_async_copy(x_hbm.at[g], x_vmem.at[pl.ds(0, TILE_ROWS)], sems.at[0]),
                pltpu.make_async_copy(slot_hbm.at[g], slot_smem, sems.at[1]),
                pltpu.make_async_copy(w_hbm.at[g], w_smem, sems.at[2]))

    def weight_copies(e, buf):
        return (pltpu.make_async_copy(wg_hbm.at[layer, e], wgf_ref.at[buf], wsems.at[buf, 0]),
                pltpu.make_async_copy(wu_hbm.at[layer, e], wuf_ref.at[buf], wsems.at[buf, 1]),
                pltpu.make_async_copy(wd_hbm.at[layer, e], wdf_ref.at[buf], wsems.at[buf, 2]))

    for c in group_copies():
        c.start()
    turn_ref[0] = 0
    for c in weight_copies(blk_exp_ref[g * MOE_NB], 0):
        c.start()
    acc_vmem[...] = jnp.zeros_like(acc_vmem)
    x_vmem[pl.ds(TILE_ROWS, ACC_TILE_ROWS - TILE_ROWS), :] = jnp.zeros(
        (ACC_TILE_ROWS - TILE_ROWS, LANES), F32)
    yt1_ref[...] = jnp.zeros_like(yt1_ref)
    for c in group_copies():
        c.wait()
    gather(0, gat0_ref)

    def step(j, gat_cur, gat_next, yt_cur, yt_prev):
        idx = g * MOE_NB + jnp.minimum(j, MOE_NB - 1)
        e_cur = blk_exp_ref[idx]
        e_prev = blk_exp_ref[jnp.maximum(idx - 1, 0)]

        @pl.when((j == 0) | (e_cur != e_prev))
        def _():
            buf = turn_ref[0]
            for c in weight_copies(e_cur, buf):
                c.wait()
            wgb_ref[...] = wgf_ref[buf].astype(BF16)
            wub_ref[...] = wuf_ref[buf].astype(BF16)
            wdb_ref[...] = wdf_ref[buf].astype(BF16)
            e_next = next_exp_ref[g * N_EXPERTS + e_cur]

            @pl.when(e_next >= 0)
            def _():
                for c in weight_copies(e_next, 1 - buf):
                    c.start()

            turn_ref[0] = 1 - buf

        gather(j + 1, gat_next)
        expert_mlp(gat_cur, yt_cur)
        combine(jnp.maximum(j - 1, 0), yt_prev)

    def step_pair(i, carry):
        step(2 * i, gat0_ref, gat1_ref, yt0_ref, yt1_ref)
        step(2 * i + 1, gat1_ref, gat0_ref, yt1_ref, yt0_ref)
        return carry

    lax.fori_loop(0, (n_used_ref[g] + 2) // 2, step_pair, 0)

    tile0 = g * TILES_PER_GROUP
    in_ctx = g < N_CTX_GROUPS

    def base_copy(l, buf):
        return pltpu.make_async_copy(
            base_hbm.at[pl.ds((tile0 + l) * ROW_TILE, ROW_TILE)], bbuf_ref.at[buf], fsems.at[0, buf])

    def ctx_copy(l, buf):
        return pltpu.make_async_copy(
            ybuf_ref.at[buf], yc_hbm.at[pl.ds((tile0 + l) * ROW_TILE, ROW_TILE)], fsems.at[1, buf])

    def lat_copy(l, buf):
        return pltpu.make_async_copy(
            ybuf_ref.at[buf], yl_hbm.at[pl.ds((tile0 + l - N_CTX_TILES) * ROW_TILE, ROW_TILE)],
            fsems.at[1, buf])

    def y_copy(l, buf, action):
        @pl.when(in_ctx)
        def _():
            action(ctx_copy(l, buf))

        @pl.when(jnp.logical_not(in_ctx))
        def _():
            action(lat_copy(l, buf))

    base_copy(0, 0).start()
    for l in range(TILES_PER_GROUP):
        buf = l % 2
        if l + 1 < TILES_PER_GROUP:
            base_copy(l + 1, 1 - buf).start()
        base_copy(l, buf).wait()
        if l >= 2:
            y_copy(l - 2, buf, lambda c: c.wait())
        routed = jnp.concatenate(
            [acc_vmem[pl.ds(l * ROW_TILE * SUBLANES + c, ROW_TILE, stride=LANE_CHUNKS), :]
             for c in range(LANE_CHUNKS)], axis=1)
        gate = gate_ref[pl.ds(_cond_of_tile(tile0 + l), 1), :]
        ybuf_ref[buf] = bbuf_ref[buf] + gate * routed
        y_copy(l, buf, lambda c: c.start())
    for l in range(TILES_PER_GROUP - 2, TILES_PER_GROUP):
        y_copy(l, l % 2, lambda c: c.wait())


N_WEIGHT_BUFS = 2


def _expert_call(blk_exp, n_used, next_exp, slots, ws, x_tiles, wg, wu, wd, base, gate, layer):
    any_spec = pl.BlockSpec(memory_space=pl.ANY)
    return pl.pallas_call(
        functools.partial(_expert_kernel, layer=layer),
        out_shape=STREAM_SHAPES,
        grid_spec=pltpu.PrefetchScalarGridSpec(
            num_scalar_prefetch=3,
            grid=(N_GROUPS,),
            in_specs=[any_spec] * 7 + [pl.BlockSpec((COND_ROWS, D), lambda g, *_: (0, 0))],
            out_specs=(any_spec, any_spec),
            scratch_shapes=[
                pltpu.VMEM((ACC_TILE_ROWS, LANES), F32),
                pltpu.VMEM((ACC_TILE_ROWS, LANES), F32),
                pltpu.VMEM((MOE_BM * SUBLANES, LANES), F32),
                pltpu.VMEM((MOE_BM * SUBLANES, LANES), F32),
                pltpu.VMEM((MOE_BM * SUBLANES, LANES), F32),
                pltpu.VMEM((MOE_BM * SUBLANES, LANES), F32),
                pltpu.VMEM((N_WEIGHT_BUFS, D, D_EXPERT), F32),
                pltpu.VMEM((N_WEIGHT_BUFS, D, D_EXPERT), F32),
                pltpu.VMEM((N_WEIGHT_BUFS, D_EXPERT, D), F32),
                pltpu.VMEM((D, D_EXPERT), BF16),
                pltpu.VMEM((D, D_EXPERT), BF16),
                pltpu.VMEM((D_EXPERT, D), BF16),
                pltpu.VMEM((2, ROW_TILE, D), F32),
                pltpu.VMEM((2, ROW_TILE, D), F32),
                pltpu.SMEM((SLOT_ROWS, LANES), I32),
                pltpu.SMEM((N_ASSIGN,), F32),
                pltpu.SMEM((1,), I32),
                pltpu.SemaphoreType.DMA((3,)),
                pltpu.SemaphoreType.DMA((N_WEIGHT_BUFS, 3)),
                pltpu.SemaphoreType.DMA((2, 2)),
            ]),
        compiler_params=_params(("arbitrary",)),
        name="moe_experts",
    )(blk_exp, n_used, next_exp, slots, ws, x_tiles, wg, wu, wd, base, gate)


BM_SHIFT = MOE_BM.bit_length() - 1


def _plan_kernel(cnt_ref, start_ref, n_used_ref, blk_exp_ref, next_exp_ref):
    for g in range(N_GROUPS):
        def expert(e, carry):
            run, last = carry
            blocks = (cnt_ref[g * N_EXPERTS + e] + MOE_BM - 1) >> BM_SHIFT
            start_ref[g * N_EXPERTS + e] = run << BM_SHIFT

            def mark(b, c):
                blk_exp_ref[g * MOE_NB + b] = e
                return c

            lax.fori_loop(run, run + blocks, mark, 0)
            return run + blocks, jnp.where(blocks > 0, e, last)

        used, last = lax.fori_loop(0, N_EXPERTS, expert, (0, 0))
        n_used_ref[g] = used

        def mark_unused(b, c):
            blk_exp_ref[g * MOE_NB + b] = last
            return c

        lax.fori_loop(used, MOE_NB, mark_unused, 0)

        def link(i, nxt):
            e = N_EXPERTS - 1 - i
            next_exp_ref[g * N_EXPERTS + e] = nxt
            return jnp.where(cnt_ref[g * N_EXPERTS + e] > 0, e, nxt)

        lax.fori_loop(0, N_EXPERTS, link, -1)


def _plan_call(counts):
    smem = pl.BlockSpec(memory_space=pltpu.SMEM)
    return pl.pallas_call(
        _plan_kernel,
        out_shape=(jax.ShapeDtypeStruct((N_GROUPS * N_EXPERTS,), I32),
                   jax.ShapeDtypeStruct((N_GROUPS,), I32),
                   jax.ShapeDtypeStruct((N_GROUPS * MOE_NB,), I32),
                   jax.ShapeDtypeStruct((N_GROUPS * N_EXPERTS,), I32)),
        in_specs=[smem],
        out_specs=(smem, smem, smem, smem),
        name="moe_plan",
    )(counts)


def _moe_layer(y, ada, norm_ffn, router_w, router_bias, wg, wu, wd, sg, su, sd, layer):
    x_tiles, base, keys, ws, cnts = _router_call(
        y, ada, norm_ffn, router_w, router_bias, sg, su, sd, layer)
    starts, n_used, blk_exp, next_exp = _plan_call(cnts[:, :, 0].astype(I32).reshape(-1))
    pos = _pos_call(starts, keys)
    slots = jnp.stack([
        _slot_call(pos[:, g * MOE_GROUP:(g + 1) * MOE_GROUP].reshape(-1))
        for g in range(N_GROUPS)]).reshape(N_GROUPS, SLOT_ROWS, LANES)
    ws_g = ws.reshape(TOP_K, N_GROUPS, MOE_GROUP).transpose(1, 0, 2).reshape(N_GROUPS, N_ASSIGN)
    gate = ada[layer, :, N_ADA - 1, :]
    return _expert_call(blk_exp, n_used, next_exp, slots, ws_g, x_tiles, wg, wu, wd, base, gate, layer)


def kernel(x_prompt, x_sample, cache_k, cache_v, c, c_ctx, ada_w, ada_b, norm_mix, norm_ffn,
           conv_w1, conv_b1, conv_dw, conv_dw_b, conv_norm, conv_w2, conv_b2,
           attn_wqkv, attn_q_norm, attn_k_norm, attn_sink, attn_wo,
           router_w, router_bias, exp_w_gate, exp_w_up, exp_w_down,
           sh_w_gate, sh_w_up, sh_w_down):
    y = (x_prompt.reshape(TP, D), x_sample.reshape(TS, D))
    cond = jnp.concatenate(
        [c_ctx[None, :], c, jnp.zeros((COND_ROWS - N_COND, D), F32)], axis=0)
    ada = _ada_call(cond, ada_w, ada_b).reshape(DEPTH, COND_ROWS, N_ADA, D)

    new_k = new_v = None
    for layer in range(DEPTH):
        j = layer // 2
        if layer % 2 == 0:
            u = _conv_in_call(y, ada, norm_mix, conv_w1, conv_b1, layer, j)
            y = _conv_out_call(u, y, ada, conv_dw, conv_dw_b, conv_norm, conv_w2, conv_b2, layer, j)
        else:
            q, k, v, kt, vt = _qkv_call(y, ada, norm_mix, attn_wqkv, attn_q_norm, attn_k_norm, layer, j)
            cache_shape = (N_CTX_SEQ, 1, N_KV, HEAD_DIM, CTX_LEN)
            new_k = kt.reshape(cache_shape).transpose(0, 1, 4, 2, 3)
            new_v = vt.reshape(cache_shape).transpose(0, 1, 4, 2, 3)
            y = (_attn_ctx_call(q, k, v, attn_sink, attn_wo, y[0], ada, layer, j),
                 _attn_lat_call(q, k, v, cache_k, cache_v, attn_sink, attn_wo, y[1], ada, layer, j))
        y = _moe_layer(y, ada, norm_ffn, router_w, router_bias, exp_w_gate, exp_w_up, exp_w_down,
                       sh_w_gate, sh_w_up, sh_w_down, layer)

    y_p = y[0].reshape(N_CTX_SEQ, CTX_LEN, D)
    y_s = y[1].reshape(N_LAT_SEQ, LAT_LEN, D)
    return (y_p, y_s, new_k, new_v)
```
